```python
import math
import jax
import jax.numpy as jnp
from jax import lax
import numpy as np

D_MODEL = 1024
BATCH = 8
SEQ = 8192
DEPTH = 1
DEC_BATCH = 8
DEC_SEQ = 32
PAST_LEN = 4096

CHUNK = 64
N_HEADS_A = 8
HEAD_DIM_A = 64
N_HEADS_IDX = 4
HEAD_DIM_IDX = 64
TOPK_MAX = 256
Q_BLOCK = CHUNK
NUM_BUCKETS = 32
MAX_DISTANCE = 1024
N_HEADS_R = 8
KEY_DIM_R = 64
VAL_DIM_R = 128
ROPE_BASE = 10000.0
D_FF = -(-8 * D_MODEL // (3 * 256)) * 256
ALPHA = (2.0 * DEPTH) ** 0.25
BETA = (8.0 * DEPTH) ** -0.25
LN_EPS = 1e-5
GN_EPS = 1e-6
W_A = N_HEADS_A * HEAD_DIM_A
W_IQ = N_HEADS_IDX * HEAD_DIM_IDX
W_RQK = N_HEADS_R * KEY_DIM_R
W_RV = N_HEADS_R * VAL_DIM_R
SPLIT_SIZES = (W_A, W_A, W_A, W_IQ, HEAD_DIM_IDX, N_HEADS_IDX, W_RQK, W_RQK, W_RV, W_RV, D_MODEL, D_MODEL)
D_IN = sum(SPLIT_SIZES)

kernel_name = 'dsa_retention_streaming_encoder'


def layer_norm(x, g, b, eps=LN_EPS):
    xf = x.astype(jnp.float32)
    mu = jnp.mean(xf, axis=-1, keepdims=True)
    var = jnp.mean(jnp.square(xf - mu), axis=-1, keepdims=True)
    y = (xf - mu) * lax.rsqrt(var + eps) * g.astype(jnp.float32) + b.astype(jnp.float32)
    return y.astype(x.dtype)


def project(x, w_in):
    h = jnp.einsum('btd,de->bte', x, w_in)
    offsets = tuple(int(v) for v in np.cumsum(SPLIT_SIZES)[:-1])
    return jnp.split(h, offsets, axis=-1)


def rotary(x, pos):
    half = x.shape[-1] // 2
    inv_freq = ROPE_BASE ** (-jnp.arange(half, dtype=jnp.float32) / half)
    ang = pos.astype(jnp.float32)[:, None] * inv_freq[None, :]
    cos = jnp.cos(ang)[None, :, None, :]
    sin = jnp.sin(ang)[None, :, None, :]
    xf = x.astype(jnp.float32)
    x1, x2 = xf[..., :half], xf[..., half:]
    return jnp.concatenate([x1 * cos - x2 * sin, x1 * sin + x2 * cos], axis=-1).astype(x.dtype)


def t5_bucket(rel):
    nb = NUM_BUCKETS // 2
    max_exact = nb // 2
    ret = jnp.where(rel > 0, nb, 0)
    n = jnp.abs(rel)
    nf = jnp.maximum(n, max_exact).astype(jnp.float32)
    large = max_exact + (jnp.log(nf / max_exact) / math.log(MAX_DISTANCE / max_exact) * (nb - max_exact)).astype(jnp.int32)
    large = jnp.minimum(large, nb - 1)
    return ret + jnp.where(n < max_exact, n, large)


def dsa_attend(q, qi, w_idx, qpos, k, v, k_idx, t5_bias, n_top):
    L = k.shape[1]
    kpos = jnp.arange(L, dtype=jnp.int32)
    rel = jax.nn.relu(jnp.einsum('bqhd,bsd->bqhs', qi, k_idx) * (HEAD_DIM_IDX ** -0.5))
    score = jnp.einsum('bqhs,bqh->bqs', rel, w_idx).astype(jnp.float32)
    admissible = (kpos[None, :] // CHUNK) <= (qpos[:, None] // CHUNK)
    score = jnp.where(admissible[None], score, -jnp.inf)
    top, idx = lax.top_k(score, n_top)
    valid = jnp.isfinite(top)
    gather = jax.vmap(lambda a, i: a[i])
    k_sel = gather(k, idx)
    v_sel = gather(v, idx)
    bias = t5_bias[t5_bucket(idx - qpos[None, :, None])]
    logits = jnp.einsum('bqhd,bqnhd->bqhn', q, k_sel).astype(jnp.float32) * (HEAD_DIM_A ** -0.5)
    logits = logits + jnp.moveaxis(bias, 3, 2).astype(jnp.float32)
    logits = jnp.where(valid[:, :, None, :], logits, -jnp.inf)
    p = jax.nn.softmax(logits, axis=-1).astype(v.dtype)
    return jnp.einsum('bqhn,bqnhd->bqhd', p, v_sel)


def retention_chunk(q, k, v, S, log_gamma):
    C = q.shape[2]
    n = jnp.arange(C, dtype=jnp.float32)
    diff = n[:, None] - n[None, :]
    lg = log_gamma[:, None, None]
    D = jnp.where(diff >= 0, jnp.exp(lg * jnp.maximum(diff, 0.0)), 0.0)
    qf, kf, vf = q.astype(jnp.float32), k.astype(jnp.float32), v.astype(jnp.float32)
    inner = jnp.einsum('bhnd,bhmd->bhnm', qf, kf) * D[None]
    q_dec = qf * jnp.exp(log_gamma[:, None] * (n + 1.0))[None, :, :, None]
    o = jnp.einsum('bhnm,bhme->bhne', inner, vf) + jnp.einsum('bhnd,bhde->bhne', q_dec, S)
    k_dec = kf * jnp.exp(log_gamma[:, None] * (C - 1.0 - n))[None, :, :, None]
    S_new = jnp.exp(log_gamma * C)[None, :, None, None] * S + jnp.einsum('bhmd,bhme->bhde', k_dec, vf)
    return o, S_new


def retention_output(o, gate, gn_g):
    B, H, T, dv = o.shape
    mu = jnp.mean(o, axis=-1, keepdims=True)
    var = jnp.mean(jnp.square(o - mu), axis=-1, keepdims=True)
    on = (o - mu) * lax.rsqrt(var + GN_EPS)
    on = on.transpose(0, 2, 1, 3).reshape(B, T, H * dv) * gn_g.astype(jnp.float32)
    return on.astype(gate.dtype) * jax.nn.silu(gate)


def split_inputs(x, pos, w_in, idx_g, idx_b):
    B, T, _ = x.shape
    qa, ka, va, qi, ki, wi, qr, kr, vr, gr, g_a, g_r = project(x, w_in)
    qa = qa.reshape(B, T, N_HEADS_A, HEAD_DIM_A)
    ka = ka.reshape(B, T, N_HEADS_A, HEAD_DIM_A)
    va = va.reshape(B, T, N_HEADS_A, HEAD_DIM_A)
    qi = qi.reshape(B, T, N_HEADS_IDX, HEAD_DIM_IDX)
    ki = layer_norm(ki, idx_g, idx_b)
    wi = wi * (N_HEADS_IDX ** -0.5)
    qr = rotary(qr.reshape(B, T, N_HEADS_R, KEY_DIM_R), pos).transpose(0, 2, 1, 3)
    kr = (rotary(kr.reshape(B, T, N_HEADS_R, KEY_DIM_R), pos) * (KEY_DIM_R ** -0.5)).transpose(0, 2, 1, 3)
    vr = vr.reshape(B, T, N_HEADS_R, VAL_DIM_R).transpose(0, 2, 1, 3)
    return qa, ka, va, qi, ki, wi, qr, kr, vr, gr, g_a, g_r


def finish_layer(x, a, o_ret, gr, g_a, g_r, ret_gn_g, w_pa, w_pr, w_o, ln1_g, ln1_b, w_gate, w_up, w_down, ln2_g, ln2_b):
    r = retention_output(o_ret, gr, ret_gn_g)
    merged = jax.nn.sigmoid(g_a) * (a @ w_pa) + jax.nn.sigmoid(g_r) * (r @ w_pr)
    x1 = layer_norm(ALPHA * x + merged @ w_o, ln1_g, ln1_b)
    h = jax.nn.silu(x1 @ w_gate) * (x1 @ w_up)
    return layer_norm(ALPHA * x1 + h @ w_down, ln2_g, ln2_b)


def prompt_sparse_attention(qa, ka, va, qi, ki, wi, t5_bias):
    B, T = qa.shape[:2]
    nb = T // Q_BLOCK
    n_top = min(TOPK_MAX, T // 4)

    def blockify(a):
        return jnp.moveaxis(a.reshape((B, nb, Q_BLOCK) + a.shape[2:]), 1, 0)

    qpos = jnp.arange(T, dtype=jnp.int32).reshape(nb, Q_BLOCK)

    def one_block(xs):
        q_b, qi_b, w_b, pos_b = xs
        return dsa_attend(q_b, qi_b, w_b, pos_b, ka, va, ki, t5_bias, n_top)

    out = lax.map(one_block, (blockify(qa), blockify(qi), blockify(wi), qpos))
    return jnp.moveaxis(out, 0, 1).reshape(B, T, W_A)


def prompt_retention(qr, kr, vr, log_gamma):
    B, H, T, _ = qr.shape
    nc = T // CHUNK

    def chunkify(a):
        return jnp.moveaxis(a.reshape(B, H, nc, CHUNK, a.shape[-1]), 2, 0)

    def step(S, xs):
        q_c, k_c, v_c = xs
        o_c, S_new = retention_chunk(q_c, k_c, v_c, S, log_gamma)
        return S_new, o_c

    S0 = jnp.zeros((B, H, KEY_DIM_R, VAL_DIM_R), jnp.float32)
    S_fin, o = lax.scan(step, S0, (chunkify(qr), chunkify(kr), chunkify(vr)))
    o = jnp.moveaxis(o, 0, 2).reshape(B, H, T, VAL_DIM_R)
    return o, S_fin


def setup_inputs(seed: int = 0) -> dict:
    key = jax.random.key(seed)
    ks = jax.random.split(key, 24)
    nrm = jax.random.normal
    f32 = jnp.float32
    return {
        'x_prompt': nrm(ks[0], (BATCH, SEQ, D_MODEL), f32),
        'x_sample': nrm(ks[1], (DEC_BATCH, DEC_SEQ, D_MODEL), f32),
        'cache_k': nrm(ks[2], (DEPTH, DEC_BATCH, PAST_LEN, N_HEADS_A, HEAD_DIM_A), f32),
        'cache_v': nrm(ks[3], (DEPTH, DEC_BATCH, PAST_LEN, N_HEADS_A, HEAD_DIM_A), f32),
        'cache_idx_k': nrm(ks[4], (DEPTH, DEC_BATCH, PAST_LEN, HEAD_DIM_IDX), f32),
        'state_ret': 0.5 * nrm(ks[5], (DEPTH, DEC_BATCH, N_HEADS_R, KEY_DIM_R, VAL_DIM_R), f32),
        'w_in': nrm(ks[6], (DEPTH, D_MODEL, D_IN), f32) * D_MODEL ** -0.5,
        'idx_k_norm_g': 1.0 + 0.02 * nrm(ks[7], (DEPTH, HEAD_DIM_IDX), f32),
        'idx_k_norm_b': 0.02 * nrm(ks[8], (DEPTH, HEAD_DIM_IDX), f32),
        't5_bias': 0.5 * nrm(ks[9], (NUM_BUCKETS, N_HEADS_A), f32),
        'ret_gn_g': 1.0 + 0.02 * nrm(ks[10], (DEPTH, W_RV), f32),
        'w_pa': nrm(ks[11], (DEPTH, W_A, D_MODEL), f32) * W_A ** -0.5,
        'w_pr': nrm(ks[12], (DEPTH, W_RV, D_MODEL), f32) * W_RV ** -0.5,
        'w_o': nrm(ks[13], (DEPTH, D_MODEL, D_MODEL), f32) * (D_MODEL ** -0.5 * BETA),
        'ln1_g': 1.0 + 0.02 * nrm(ks[14], (DEPTH, D_MODEL), f32),
        'ln1_b': 0.02 * nrm(ks[15], (DEPTH, D_MODEL), f32),
        'w_gate': nrm(ks[16], (DEPTH, D_MODEL, D_FF), f32) * D_MODEL ** -0.5,
        'w_up': nrm(ks[17], (DEPTH, D_MODEL, D_FF), f32) * D_MODEL ** -0.5,
        'w_down': nrm(ks[18], (DEPTH, D_FF, D_MODEL), f32) * (D_FF ** -0.5 * BETA),
        'ln2_g': 1.0 + 0.02 * nrm(ks[19], (DEPTH, D_MODEL), f32),
        'ln2_b': 0.02 * nrm(ks[20], (DEPTH, D_MODEL), f32),
    }


def reference(x_prompt, x_sample, cache_k, cache_v, cache_idx_k, state_ret, w_in, idx_k_norm_g, idx_k_norm_b,
              t5_bias, ret_gn_g, w_pa, w_pr, w_o, ln1_g, ln1_b, w_gate, w_up, w_down, ln2_g, ln2_b):
    log_gamma = jnp.log1p(-jnp.exp2(-5.0 - jnp.arange(N_HEADS_R, dtype=jnp.float32)))
    yp, ys = x_prompt, x_sample
    kp_l, vp_l, ikp_l, sp_l = [], [], [], []
    ks_l, vs_l, iks_l, ss_l = [], [], [], []
    for l in range(DEPTH):
        lw = (ret_gn_g[l], w_pa[l], w_pr[l], w_o[l], ln1_g[l], ln1_b[l], w_gate[l], w_up[l], w_down[l], ln2_g[l], ln2_b[l])

        T = yp.shape[1]
        pos_p = jnp.arange(T, dtype=jnp.int32)
        qa, ka, va, qi, ki, wi, qr, kr, vr, gr, g_a, g_r = split_inputs(yp, pos_p, w_in[l], idx_k_norm_g[l], idx_k_norm_b[l])
        a_p = prompt_sparse_attention(qa, ka, va, qi, ki, wi, t5_bias)
        o_p, s_p = prompt_retention(qr, kr, vr, log_gamma)
        yp_new = finish_layer(yp, a_p, o_p, gr, g_a, g_r, *lw)
        kp_l.append(ka)
        vp_l.append(va)
        ikp_l.append(ki)
        sp_l.append(s_p)

        past = cache_k.shape[2]
        Bs, Ts = ys.shape[:2]
        pos_s = past + jnp.arange(Ts, dtype=jnp.int32)
        qa_s, ka_s, va_s, qi_s, ki_s, wi_s, qr_s, kr_s, vr_s, gr_s, g_a_s, g_r_s = split_inputs(
            ys, pos_s, w_in[l], idx_k_norm_g[l], idx_k_norm_b[l])
        k_all = jnp.concatenate([cache_k[l], ka_s.astype(cache_k.dtype)], axis=1)
        v_all = jnp.concatenate([cache_v[l], va_s.astype(cache_v.dtype)], axis=1)
        ki_all = jnp.concatenate([cache_idx_k[l], ki_s.astype(cache_idx_k.dtype)], axis=1)
        L = past + Ts
        a_s = dsa_attend(qa_s, qi_s, wi_s, pos_s, k_all, v_all, ki_all, t5_bias, min(TOPK_MAX, L // 4)).reshape(Bs, Ts, W_A)
        o_s, s_s = retention_chunk(qr_s, kr_s, vr_s, state_ret[l].astype(jnp.float32), log_gamma)
        ys_new = finish_layer(ys, a_s, o_s, gr_s, g_a_s, g_r_s, *lw)
        ks_l.append(ka_s)
        vs_l.append(va_s)
        iks_l.append(ki_s)
        ss_l.append(s_s.astype(state_ret.dtype))

        yp, ys = yp_new, ys_new

    return (yp, ys,
            jnp.stack(kp_l), jnp.stack(vp_l), jnp.stack(ikp_l), jnp.stack(sp_l),
            jnp.stack(ks_l), jnp.stack(vs_l), jnp.stack(iks_l), jnp.stack(ss_l))
```

```python
import functools
import math

import numpy as np
import jax
import jax.numpy as jnp
from jax import lax
from jax.experimental import pallas as pl
from jax.experimental.pallas import tpu as pltpu

D_MODEL = 1024
CHUNK = 64
N_HEADS_A = 8
HEAD_DIM_A = 64
N_HEADS_IDX = 4
HEAD_DIM_IDX = 64
TOPK_MAX = 256
NUM_BUCKETS = 32
MAX_DISTANCE = 1024
N_HEADS_R = 8
KEY_DIM_R = 64
VAL_DIM_R = 128
ROPE_BASE = 10000.0
D_FF = 2816
DEPTH = 1
ALPHA = (2.0 * DEPTH) ** 0.25
LN_EPS = 1e-5
GN_EPS = 1e-6

W_A = N_HEADS_A * HEAD_DIM_A
W_IQ = N_HEADS_IDX * HEAD_DIM_IDX
W_RQK = N_HEADS_R * KEY_DIM_R
W_RV = N_HEADS_R * VAL_DIM_R
SPLIT_SIZES = (W_A, W_A, W_A, W_IQ, HEAD_DIM_IDX, N_HEADS_IDX, W_RQK, W_RQK, W_RV, W_RV, D_MODEL, D_MODEL)

LANES = 128
VMEM_LIMIT_BYTES = 60 * 1024 * 1024

OFF_QA = 0
OFF_KA = OFF_QA + W_A
OFF_VA = OFF_KA + W_A
OFF_QI = OFF_VA + W_A
OFF_KI4 = OFF_QI + W_IQ
OFF_WI = OFF_KI4 + W_IQ
OFF_QR = OFF_WI + LANES
OFF_KR = OFF_QR + W_RQK
OFF_VR = OFF_KR + W_RQK
OFF_GR = OFF_VR + W_RV
OFF_GA = OFF_GR + W_RV
OFF_GG = OFF_GA + D_MODEL
W_TOTAL = OFF_GG + D_MODEL

INT_MIN = -(2 ** 31)
NEG_BIG = -1e30
BF16 = jnp.bfloat16
F32 = jnp.float32


def _dot(a, b):
    return jnp.dot(a, b, preferred_element_type=F32)


def _dot_nt(a, b):
    return lax.dot_general(a, b, (((1,), (1,)), ((), ())), preferred_element_type=F32)


def _dot_tn(a, b):
    return lax.dot_general(a, b, (((0,), (0,)), ((), ())), preferred_element_type=F32)


def _layer_norm(z, g, b):
    mu = jnp.mean(z, axis=-1, keepdims=True)
    d = z - mu
    var = jnp.mean(d * d, axis=-1, keepdims=True)
    return d * lax.rsqrt(var + LN_EPS) * g + b


def _const_spec(shape):
    nd = len(shape)
    return pl.BlockSpec(shape, lambda *_: (0,) * nd, pipeline_mode=pl.Buffered(1))


def _proj_kernel(x_ref, w_ref, cos_ref, sin_ref, g4_ref, b4_ref,
                 qa_o, kf_o, kb_o, vf_o, vb_o, qi_o, kif_o, ki4_o, wi_o,
                 qr_o, kr_o, vr_o, gr_o, ga_o, gg_o):
    xb = x_ref[...].astype(BF16)

    def seg(off, n):
        return _dot(xb, w_ref[:, off:off + n])

    qa_o[...] = (seg(OFF_QA, W_A) * (HEAD_DIM_A ** -0.5)).astype(BF16)
    k = seg(OFF_KA, W_A)
    kf_o[...] = k
    kb_o[...] = k.astype(BF16)
    v = seg(OFF_VA, W_A)
    vf_o[...] = v
    vb_o[...] = v.astype(BF16)
    qi_o[...] = seg(OFF_QI, W_IQ).astype(BF16)

    ki = seg(OFF_KI4, W_IQ)
    first = lax.broadcasted_iota(jnp.int32, ki.shape, 1) < HEAD_DIM_IDX
    inv_n = 1.0 / HEAD_DIM_IDX
    mu = jnp.sum(jnp.where(first, ki, 0.0), axis=-1, keepdims=True) * inv_n
    d = ki - mu
    var = jnp.sum(jnp.where(first, d * d, 0.0), axis=-1, keepdims=True) * inv_n
    kin = d * lax.rsqrt(var + LN_EPS) * g4_ref[...] + b4_ref[...]
    kif_o[...] = kin[:, :HEAD_DIM_IDX]
    ki4_o[...] = kin.astype(BF16)

    wi_o[...] = seg(OFF_WI, LANES) * ((N_HEADS_IDX ** -0.5) * (HEAD_DIM_IDX ** -0.5))

    for off, o_ref, scale in ((OFF_QR, qr_o, 1.0), (OFF_KR, kr_o, KEY_DIM_R ** -0.5)):
        h = seg(off, W_RQK)
        for j in range(W_RQK // LANES):
            xj = h[:, j * LANES:(j + 1) * LANES]
            rj = pltpu.roll(xj, LANES // 2, 1)
            oj = xj * cos_ref[...] + rj * sin_ref[...]
            o_ref[:, j * LANES:(j + 1) * LANES] = (oj * scale).astype(BF16)

    vr_o[...] = seg(OFF_VR, W_RV).astype(BF16)
    gr_o[...] = seg(OFF_GR, W_RV).astype(BF16)
    ga_o[...] = seg(OFF_GA, D_MODEL).astype(BF16)
    gg_o[...] = seg(OFF_GG, D_MODEL).astype(BF16)


def _proj(x2d, w_all, cos_t, sin_t, g4, b4, *, tm, n_pos_blocks):
    m = x2d.shape[0]
    grid = (m // tm,)

    def row(width, dtype):
        return jax.ShapeDtypeStruct((m, width), dtype)

    def rspec(width):
        return pl.BlockSpec((tm, width), lambda i: (i, 0))

    pos_spec = pl.BlockSpec((tm, LANES), lambda i: (i % n_pos_blocks, 0))
    out_shapes = (
        row(W_A, BF16), row(W_A, F32), row(W_A, BF16), row(W_A, F32), row(W_A, BF16),
        row(W_IQ, BF16), row(HEAD_DIM_IDX, F32), row(W_IQ, BF16), row(LANES, F32),
        row(W_RQK, BF16), row(W_RQK, BF16), row(W_RV, BF16), row(W_RV, BF16),
        row(D_MODEL, BF16), row(D_MODEL, BF16))
    out_specs = tuple(rspec(s.shape[1]) for s in out_shapes)
    return pl.pallas_call(
        _proj_kernel,
        grid=grid,
        in_specs=[rspec(D_MODEL), _const_spec((D_MODEL, W_TOTAL)), pos_spec, pos_spec,
                  _const_spec((1, W_IQ)), _const_spec((1, W_IQ))],
        out_specs=out_specs,
        out_shape=out_shapes,
        compiler_params=pltpu.CompilerParams(
            dimension_semantics=("arbitrary",), vmem_limit_bytes=VMEM_LIMIT_BYTES),
        name="proj",
    )(x2d, w_all, cos_t, sin_t, g4, b4)


def _attn_kernel(qa_ref, qi_ref, wi_ref, kb_ref, vb_ref, ki4_ref, bias_ref, tri_ref, o_ref,
                 key_scr, m_scr, l_scr, acc_scr, *, qb, sl, n_keys, pos0, near_off, ktop):
    i = pl.program_id(1)
    t0 = pos0 + i * qb
    lblk = jnp.minimum(n_keys, ((t0 + qb - 1) // CHUNK + 1) * CHUNK)
    nslab = (lblk + sl - 1) // sl
    nrep = sl // LANES

    qi = qi_ref[...].astype(F32)
    head_of_lane = lax.shift_right_logical(
        lax.broadcasted_iota(jnp.int32, qi.shape, 1), int(math.log2(HEAD_DIM_IDX)))
    lhs4 = jnp.concatenate(
        [jnp.where(head_of_lane == h, qi, 0.0).astype(BF16) for h in range(N_HEADS_IDX)], axis=0)
    wfull = wi_ref[...]
    wb = [jnp.broadcast_to(wfull[:, HEAD_DIM_IDX + h:HEAD_DIM_IDX + h + 1], (qb, sl))
          for h in range(N_HEADS_IDX)]
    rowpos = t0 + lax.broadcasted_iota(jnp.int32, (qb, sl), 0)
    qlim = jnp.minimum((lax.shift_right_logical(rowpos, int(math.log2(CHUNK))) + 1) * CHUNK, n_keys)
    lane_idx = lax.broadcasted_iota(jnp.int32, (qb, sl), 1)

    def a_body(j, carry):
        s0 = pl.multiple_of(j * sl, sl)
        kis = ki4_ref[pl.ds(s0, sl), :]
        r = _dot_nt(lhs4, kis)
        s = wb[0] * jnp.maximum(r[0:qb], 0.0)
        for h in range(1, N_HEADS_IDX):
            s = s + wb[h] * jnp.maximum(r[h * qb:(h + 1) * qb], 0.0)
        b = lax.bitcast_convert_type(s, jnp.int32)
        skey = jnp.where(b < 0, INT_MIN - b, b)
        skey = jnp.where(s0 + lane_idx < qlim, skey, INT_MIN)
        key_scr[j] = skey
        return carry

    lax.fori_loop(0, nslab, a_body, 0)

    def count_ge(cand_s):
        def c_body(j, acc):
            keys = key_scr[j]
            for u in range(nrep):
                acc = acc + jnp.where(keys[:, u * LANES:(u + 1) * LANES] >= cand_s, 1.0, 0.0)
            return acc
        acc = lax.fori_loop(0, nslab, c_body, jnp.zeros((qb, LANES), F32))
        return jnp.broadcast_to(jnp.sum(acc, axis=1, keepdims=True), (qb, LANES))

    def bit_body(it, t_u):
        bit = jnp.left_shift(jnp.int32(1), 31 - it)
        cand_u = t_u | bit
        cnt = count_ge(cand_u ^ INT_MIN)
        return jnp.where(cnt >= ktop, cand_u, t_u)

    t_u = lax.fori_loop(0, 32, bit_body, jnp.zeros((qb, LANES), jnp.int32))
    thr = t_u ^ INT_MIN
    n_ge_next = count_ge(jnp.where(thr == 2 ** 31 - 1, thr, thr + 1))
    n_gt = jnp.where(thr == 2 ** 31 - 1, 0.0, n_ge_next)
    need = jnp.where(thr == INT_MIN, 0.0, ktop - n_gt)

    thr_w = pltpu.repeat(thr, nrep, axis=1)
    need_w = pltpu.repeat(need, nrep, axis=1)

    def m_body(j, carry):
        keys = key_scr[j]
        eq = keys == thr_w
        eqf = jnp.where(eq, 1.0, 0.0)
        pref = _dot(eqf.astype(BF16), tri_ref[...])
        rank = pref + pltpu.repeat(carry, nrep, axis=1)
        sel = (keys > thr_w) | (eq & (rank <= need_w))
        madd = jnp.where(sel, 0.0, NEG_BIG)
        key_scr[j] = lax.bitcast_convert_type(madd, jnp.int32)
        tot = eqf[:, 0:LANES]
        for u in range(1, nrep):
            tot = tot + eqf[:, u * LANES:(u + 1) * LANES]
        return carry + jnp.broadcast_to(jnp.sum(tot, axis=1, keepdims=True), (qb, LANES))

    lax.fori_loop(0, nslab, m_body, jnp.zeros((qb, LANES), F32))

    jfar = jnp.minimum(jnp.maximum(t0 - near_off + sl - 1, 0) // sl, nslab)
    lane128 = lax.broadcasted_iota(jnp.int32, (qb, LANES), 1)
    low = lane128 < HEAD_DIM_A

    for p in range(N_HEADS_A // 2):
        q2 = qa_ref[:, p * LANES:(p + 1) * LANES].astype(F32)
        qq = jnp.concatenate([jnp.where(low, q2, 0.0).astype(BF16),
                              jnp.where(low, 0.0, q2).astype(BF16)], axis=0)
        m_scr[...] = jnp.full(m_scr.shape, NEG_BIG, F32)
        l_scr[...] = jnp.zeros(l_scr.shape, F32)
        acc_scr[...] = jnp.zeros(acc_scr.shape, F32)

        def d_body(j, carry, near, p=p, qq=qq):
            s0 = pl.multiple_of(j * sl, sl)
            k2 = kb_ref[pl.ds(s0, sl), p * LANES:(p + 1) * LANES]
            v2 = vb_ref[pl.ds(s0, sl), p * LANES:(p + 1) * LANES]
            l2 = _dot_nt(qq, k2)
            madd = lax.bitcast_convert_type(key_scr[j], F32)
            if near:
                ct = (near_off - t0 + s0) // LANES
            for e in range(2):
                logit = l2[e * qb:(e + 1) * qb] + madd
                if near:
                    bias = jnp.concatenate(
                        [bias_ref[2 * p + e, ct + u] for u in range(nrep)], axis=1)
                    logit = logit + bias
                m_prev = m_scr[e]
                m_cur = jnp.max(logit, axis=1, keepdims=True)
                m_new = jnp.maximum(m_prev, m_cur)
                alpha = jnp.exp(m_prev - m_new)
                pexp = jnp.exp(logit - pltpu.repeat(m_new, nrep, axis=1))
                psum = pexp[:, 0:LANES]
                for u in range(1, nrep):
                    psum = psum + pexp[:, u * LANES:(u + 1) * LANES]
                l_scr[e] = alpha * l_scr[e] + psum
                acc_scr[e] = alpha * acc_scr[e] + _dot(pexp.astype(BF16), v2)
                m_scr[e] = m_new
            return carry

        lax.fori_loop(0, jfar, functools.partial(d_body, near=False), 0)
        lax.fori_loop(jfar, nslab, functools.partial(d_body, near=True), 0)

        den0 = jnp.sum(l_scr[0], axis=1, keepdims=True)
        den1 = jnp.sum(l_scr[1], axis=1, keepdims=True)
        out = jnp.where(low, acc_scr[0] / den0, acc_scr[1] / den1)
        o_ref[:, p * LANES:(p + 1) * LANES] = out.astype(o_ref.dtype)


def _t5_bucket_np(rel):
    nb = NUM_BUCKETS // 2
    max_exact = nb // 2
    ret = np.where(rel > 0, nb, 0)
    n = np.abs(rel)
    nf = np.maximum(n, max_exact).astype(np.float64)
    large = max_exact + (np.log(nf / max_exact) / math.log(MAX_DISTANCE / max_exact) * (nb - max_exact)).astype(np.int64)
    large = np.minimum(large, nb - 1)
    return ret + np.where(n < max_exact, n, large)


def _t5_bucket(rel):
    nb = NUM_BUCKETS // 2
    max_exact = nb // 2
    ret = jnp.where(rel > 0, nb, 0)
    n = jnp.abs(rel)
    nf = jnp.maximum(n, max_exact).astype(F32)
    large = max_exact + (jnp.log(nf / max_exact) / math.log(MAX_DISTANCE / max_exact) * (nb - max_exact)).astype(jnp.int32)
    large = jnp.minimum(large, nb - 1)
    return ret + jnp.where(n < max_exact, n, large)


def _attention(qa, qi, wi, kb, vb, ki4, t5_bias, *, qb, sl, n_keys, pos0):
    bsz, tq, _ = qa.shape
    lp = kb.shape[1]
    assert tq % qb == 0 and lp % sl == 0 and sl % LANES == 0 and qb % 16 == 0
    ktop = min(TOPK_MAX, n_keys // 4)
    nslab_max = lp // sl

    gran = math.gcd(pos0, sl) if tq == qb else math.gcd(math.gcd(pos0, qb), sl)
    assert gran % LANES == 0
    rel_all = np.arange(-(n_keys + qb), 0)
    far_bucket = _t5_bucket_np(np.array([-(n_keys + qb)]))[0]
    sat = rel_all[_t5_bucket_np(rel_all) != far_bucket]
    n_sat = int(-sat.min()) + 1 if sat.size else 1
    near_off = -(-(sl - 1 - gran + n_sat) // gran) * gran
    width = near_off + sl
    rel_min = -(qb - 1) - near_off
    rels = jnp.arange(rel_min, sl, dtype=jnp.int32)
    tab = t5_bias[_t5_bucket(rels)] - t5_bias[far_bucket][None, :]
    idx = (np.arange(width)[None, :] - np.arange(qb)[:, None] - near_off) - rel_min
    master = jnp.transpose(tab[idx], (2, 0, 1))
    master = master.reshape(N_HEADS_A, qb, width // LANES, LANES).transpose(0, 2, 1, 3)

    tri = jnp.asarray(np.triu(np.ones((sl, sl), np.float32)), dtype=BF16)

    kern = functools.partial(_attn_kernel, qb=qb, sl=sl, n_keys=n_keys, pos0=pos0,
                             near_off=near_off, ktop=float(ktop))

    def qspec(width_):
        return pl.BlockSpec((None, qb, width_), lambda b, i: (b, i, 0))

    def kspec(width_):
        return pl.BlockSpec((None, lp, width_), lambda b, i: (b, 0, 0), pipeline_mode=pl.Buffered(1))

    return pl.pallas_call(
        kern,
        grid=(bsz, tq // qb),
        in_specs=[qspec(W_A), qspec(W_IQ), qspec(LANES), kspec(W_A), kspec(W_A), kspec(W_IQ),
                  _const_spec(master.shape), _const_spec((sl, sl))],
        out_specs=qspec(W_A),
        out_shape=jax.ShapeDtypeStruct((bsz, tq, W_A), BF16),
        scratch_shapes=[pltpu.VMEM((nslab_max, qb, sl), jnp.int32),
                        pltpu.VMEM((2, qb, LANES), F32),
                        pltpu.VMEM((2, qb, LANES), F32),
                        pltpu.VMEM((2, qb, LANES), F32)],
        compiler_params=pltpu.CompilerParams(
            dimension_semantics=("arbitrary", "arbitrary"), vmem_limit_bytes=VMEM_LIMIT_BYTES),
        name="dsa_attention",
    )(qa, qi, wi, kb, vb, ki4, master, tri)


def _ret_kernel(q_ref, k_ref, v_ref, g_ref, gn_ref, s0_ref, dm_ref, qd_ref, kd_ref, gc_ref,
                r_ref, sfin_ref, s_scr, *, n_chunks):
    c = pl.program_id(1)

    @pl.when(c == 0)
    def _():
        s_scr[...] = s0_ref[...]

    for h in range(N_HEADS_R):
        p = h // 2
        q2 = q_ref[:, p * LANES:(p + 1) * LANES].astype(F32)
        k2 = k_ref[:, p * LANES:(p + 1) * LANES]
        vh = v_ref[:, h * VAL_DIM_R:(h + 1) * VAL_DIM_R]
        s_h = s_scr[h]
        qm = (q2 * qd_ref[h, 0]).astype(BF16)
        qdec = (q2 * qd_ref[h, 1]).astype(BF16)
        kdec = (k2.astype(F32) * kd_ref[h]).astype(BF16)
        inner = _dot_nt(qm, k2) * dm_ref[h]
        o = _dot(inner.astype(BF16), vh) + _dot(qdec, s_h.astype(BF16))
        s_scr[h] = gc_ref[h] * s_h + _dot_tn(kdec, vh)
        mu = jnp.mean(o, axis=-1, keepdims=True)
        d = o - mu
        var = jnp.mean(d * d, axis=-1, keepdims=True)
        on = d * lax.rsqrt(var + GN_EPS) * gn_ref[:, h * VAL_DIM_R:(h + 1) * VAL_DIM_R]
        gate = g_ref[:, h * VAL_DIM_R:(h + 1) * VAL_DIM_R].astype(F32)
        r_ref[:, h * VAL_DIM_R:(h + 1) * VAL_DIM_R] = (on * (gate * jax.nn.sigmoid(gate))).astype(r_ref.dtype)

    @pl.when(c == n_chunks - 1)
    def _():
        sfin_ref[...] = s_scr[...]


def _pair_lane_heads():
    return (np.arange(LANES) // (KEY_DIM_R // 2)) % 2


def _retention(qr, kr, vr, gr, gn_g, s_init, log_gamma, *, chunk):
    bsz, t, _ = qr.shape
    n_chunks = t // chunk
    n = jnp.arange(chunk, dtype=F32)
    lg = log_gamma.astype(F32)
    diff = n[:, None] - n[None, :]
    dmask = jnp.where(diff >= 0, jnp.exp(lg[:, None, None] * jnp.maximum(diff, 0.0)), 0.0)
    owner = jnp.asarray(_pair_lane_heads()[None, :] == (np.arange(N_HEADS_R) % 2)[:, None], F32)
    q_dec = jnp.exp(lg[:, None] * (n + 1.0))
    k_dec = jnp.exp(lg[:, None] * (chunk - 1.0 - n))
    qd = jnp.stack([jnp.broadcast_to(owner[:, None, :], (N_HEADS_R, chunk, LANES)),
                    owner[:, None, :] * q_dec[:, :, None]], axis=1)
    kd = owner[:, None, :] * k_dec[:, :, None]
    gc = jnp.broadcast_to(jnp.exp(lg * chunk)[:, None, None], (N_HEADS_R, 1, LANES))

    def tspec(width):
        return pl.BlockSpec((None, chunk, width), lambda b, c: (b, c, 0))

    sspec = pl.BlockSpec((None, N_HEADS_R, LANES, VAL_DIM_R), lambda b, c: (b, 0, 0, 0))
    return pl.pallas_call(
        functools.partial(_ret_kernel, n_chunks=n_chunks),
        grid=(bsz, n_chunks),
        in_specs=[tspec(W_RQK), tspec(W_RQK), tspec(W_RV), tspec(W_RV), _const_spec((1, W_RV)), sspec,
                  _const_spec(dmask.shape), _const_spec(qd.shape), _const_spec(kd.shape),
                  _const_spec(gc.shape)],
        out_specs=(tspec(W_RV), sspec),
        out_shape=(jax.ShapeDtypeStruct((bsz, t, W_RV), BF16),
                   jax.ShapeDtypeStruct((bsz, N_HEADS_R, LANES, VAL_DIM_R), F32)),
        scratch_shapes=[pltpu.VMEM((N_HEADS_R, LANES, VAL_DIM_R), F32)],
        compiler_params=pltpu.CompilerParams(
            dimension_semantics=("arbitrary", "arbitrary"), vmem_limit_bytes=VMEM_LIMIT_BYTES),
        name="retention",
    )(qr, kr, vr, gr, gn_g, s_init, dmask, qd, kd, gc)


def _state_rows():
    half = KEY_DIM_R // 2
    h = np.arange(N_HEADS_R)[:, None]
    d = np.arange(KEY_DIM_R)[None, :]
    return (d // half) * (2 * half) + (h % 2) * half + (d % half)


def _finish_kernel(x_ref, a_ref, r_ref, ga_ref, gg_ref, wpa, wpr, wo, ln1g, ln1b, wg, wu, wd, ln2g, ln2b,
                   y_ref):
    x = x_ref[...]
    merged = (jax.nn.sigmoid(ga_ref[...].astype(F32)) * _dot(a_ref[...], wpa[...])
              + jax.nn.sigmoid(gg_ref[...].astype(F32)) * _dot(r_ref[...], wpr[...]))
    x1 = _layer_norm(ALPHA * x + _dot(merged.astype(BF16), wo[...]), ln1g[...], ln1b[...])
    x1b = x1.astype(BF16)
    gate = _dot(x1b, wg[...])
    hidden = gate * jax.nn.sigmoid(gate) * _dot(x1b, wu[...])
    y = _layer_norm(ALPHA * x1 + _dot(hidden.astype(BF16), wd[...]), ln2g[...], ln2b[...])
    y_ref[...] = y


def _finish(x2d, a, r, ga, gg, wpa, wpr, wo, ln1g, ln1b, wg, wu, wd, ln2g, ln2b, *, tm):
    m = x2d.shape[0]

    def rspec(width):
        return pl.BlockSpec((tm, width), lambda i: (i, 0))

    consts = (wpa, wpr, wo, ln1g, ln1b, wg, wu, wd, ln2g, ln2b)
    return pl.pallas_call(
        _finish_kernel,
        grid=(m // tm,),
        in_specs=[rspec(D_MODEL), rspec(W_A), rspec(W_RV), rspec(D_MODEL), rspec(D_MODEL)]
        + [_const_spec(c.shape) for c in consts],
        out_specs=rspec(D_MODEL),
        out_shape=jax.ShapeDtypeStruct((m, D_MODEL), F32),
        compiler_params=pltpu.CompilerParams(
            dimension_semantics=("arbitrary",), vmem_limit_bytes=VMEM_LIMIT_BYTES),
        name="finish",
    )(x2d, a, r, ga, gg, *consts)


def _relayout_w_in(w_in):
    offs = np.cumsum((0,) + SPLIT_SIZES)
    parts = [w_in[:, offs[i]:offs[i + 1]] for i in range(len(SPLIT_SIZES))]
    w_qa, w_ka, w_va, w_qi, w_ki, w_wi, w_qr, w_kr, w_vr, w_gr, w_ga, w_gg = parts
    half = KEY_DIM_R // 2
    perm = np.concatenate([
        np.arange(hh * KEY_DIM_R + part * half, hh * KEY_DIM_R + (part + 1) * half)
        for p in range(N_HEADS_R // 2) for part in (0, 1) for hh in (2 * p, 2 * p + 1)])
    w_wi_pad = jnp.pad(w_wi, ((0, 0), (HEAD_DIM_IDX, LANES - HEAD_DIM_IDX - N_HEADS_IDX)))
    w_all = jnp.concatenate(
        [w_qa, w_ka, w_va, w_qi, jnp.tile(w_ki, (1, N_HEADS_IDX)), w_wi_pad,
         w_qr[:, perm], w_kr[:, perm], w_vr, w_gr, w_ga, w_gg], axis=1)
    assert w_all.shape[1] == W_TOTAL
    return w_all.astype(BF16)


def _rotary_tables(pos):
    half = KEY_DIM_R // 2
    inv_freq = ROPE_BASE ** (-jnp.arange(half, dtype=F32) / half)
    ang = pos.astype(F32)[:, None] * inv_freq[None, :]
    cos, sin = jnp.cos(ang), jnp.sin(ang)
    return jnp.tile(cos, (1, 4)), jnp.concatenate([-sin, -sin, sin, sin], axis=1)


def _group(x, pos0, past_k, past_v, past_ki, state, w_all, g4, b4, t5_bias, log_gamma, fin_w,
           *, tm_proj, qb, sl, ret_chunk, tm_fin):
    bsz, t, _ = x.shape
    m = bsz * t
    x2d = x.reshape(m, D_MODEL)
    pos = pos0 + jnp.arange(t, dtype=jnp.int32)
    cos_t, sin_t = _rotary_tables(pos)
    n_pos_blocks = max(t // tm_proj, 1)
    if tm_proj > t:
        cos_t = jnp.tile(cos_t, (tm_proj // t, 1))
        sin_t = jnp.tile(sin_t, (tm_proj // t, 1))
    (qa, kf, kb, vf, vb, qi, kif, ki4, wi, qr, kr, vr, gr, ga, gg) = _proj(
        x2d, w_all, cos_t, sin_t, g4, b4, tm=tm_proj, n_pos_blocks=n_pos_blocks)

    def b3(a):
        return a.reshape(bsz, t, a.shape[-1])

    n_keys = t if past_k is None else past_k.shape[1] + t
    lp = -(-n_keys // sl) * sl
    kb3, vb3, ki43 = b3(kb), b3(vb), b3(ki4)
    if past_k is not None:
        pk = past_k.reshape(bsz, -1, W_A).astype(BF16)
        pv = past_v.reshape(bsz, -1, W_A).astype(BF16)
        pki = jnp.tile(past_ki, (1, 1, N_HEADS_IDX)).astype(BF16)
        kb3 = jnp.concatenate([pk, kb3], axis=1)
        vb3 = jnp.concatenate([pv, vb3], axis=1)
        ki43 = jnp.concatenate([pki, ki43], axis=1)
    if lp != n_keys:
        padw = ((0, 0), (0, lp - n_keys), (0, 0))
        kb3, vb3, ki43 = jnp.pad(kb3, padw), jnp.pad(vb3, padw), jnp.pad(ki43, padw)
    a = _attention(b3(qa), b3(qi), b3(wi), kb3, vb3, ki43, t5_bias, qb=qb, sl=sl, n_keys=n_keys, pos0=pos0)

    rows = _state_rows()
    if state is None:
        s_init = jnp.zeros((bsz, N_HEADS_R, LANES, VAL_DIM_R), F32)
    else:
        s_init = jnp.zeros((bsz, N_HEADS_R, LANES, VAL_DIM_R), F32)
        s_init = s_init.at[:, np.arange(N_HEADS_R)[:, None], rows].set(state.astype(F32))
    r, s_fin = _retention(b3(qr), b3(kr), b3(vr), b3(gr), fin_w["gn"], s_init, log_gamma, chunk=ret_chunk)
    s_out = s_fin[:, np.arange(N_HEADS_R)[:, None], rows]

    y = _finish(x2d, a.reshape(m, W_A), r.reshape(m, W_RV), ga, gg,
                fin_w["wpa"], fin_w["wpr"], fin_w["wo"], fin_w["ln1g"], fin_w["ln1b"],
                fin_w["wg"], fin_w["wu"], fin_w["wd"], fin_w["ln2g"], fin_w["ln2b"], tm=tm_fin)
    return (y.reshape(bsz, t, D_MODEL),
            kf.reshape(bsz, t, N_HEADS_A, HEAD_DIM_A), vf.reshape(bsz, t, N_HEADS_A, HEAD_DIM_A),
            kif.reshape(bsz, t, HEAD_DIM_IDX), s_out)


def kernel(x_prompt, x_sample, cache_k, cache_v, cache_idx_k, state_ret, w_in, idx_k_norm_g, idx_k_norm_b,
           t5_bias, ret_gn_g, w_pa, w_pr, w_o, ln1_g, ln1_b, w_gate, w_up, w_down, ln2_g, ln2_b):
    assert w_in.shape[0] == DEPTH
    log_gamma = jnp.log1p(-jnp.exp2(-5.0 - jnp.arange(N_HEADS_R, dtype=F32)))
    l = 0
    w_all = _relayout_w_in(w_in[l])
    g4 = jnp.tile(idx_k_norm_g[l], N_HEADS_IDX)[None, :].astype(F32)
    b4 = jnp.tile(idx_k_norm_b[l], N_HEADS_IDX)[None, :].astype(F32)
    fin_w = dict(
        gn=ret_gn_g[l][None, :].astype(F32),
        wpa=w_pa[l].astype(BF16), wpr=w_pr[l].astype(BF16), wo=w_o[l].astype(BF16),
        ln1g=ln1_g[l][None, :].astype(F32), ln1b=ln1_b[l][None, :].astype(F32),
        wg=w_gate[l].astype(BF16), wu=w_up[l].astype(BF16), wd=w_down[l].astype(BF16),
        ln2g=ln2_g[l][None, :].astype(F32), ln2b=ln2_b[l][None, :].astype(F32))

    t_p = x_prompt.shape[1]
    yp, kp, vp, ikp, sp = _group(
        x_prompt, 0, None, None, None, None, w_all, g4, b4, t5_bias, log_gamma, fin_w,
        tm_proj=min(512, t_p), qb=128, sl=512, ret_chunk=min(256, t_p), tm_fin=min(256, t_p))

    bs, t_s, _ = x_sample.shape
    past = cache_k.shape[2]
    ys, ks, vs, iks, ss = _group(
        x_sample, past, cache_k[l], cache_v[l], cache_idx_k[l], state_ret[l], w_all, g4, b4, t5_bias,
        log_gamma, fin_w, tm_proj=bs * t_s, qb=t_s, sl=512, ret_chunk=t_s, tm_fin=bs * t_s)

    return (yp, ys, kp[None], vp[None], ikp[None], sp[None],
            ks[None], vs[None], iks[None], ss.astype(state_ret.dtype)[None])
```

```python
import functools
import math

import numpy as np
import jax
import jax.numpy as jnp
from jax import lax
from jax.experimental import pallas as pl
from jax.experimental.pallas import tpu as pltpu

D_MODEL = 1024
CHUNK = 64
N_HEADS_A = 8
HEAD_DIM_A = 64
N_HEADS_IDX = 4
HEAD_DIM_IDX = 64
TOPK_MAX = 256
NUM_BUCKETS = 32
MAX_DISTANCE = 1024
N_HEADS_R = 8
KEY_DIM_R = 64
VAL_DIM_R = 128
ROPE_BASE = 10000.0
D_FF = 2816
DEPTH = 1
ALPHA = (2.0 * DEPTH) ** 0.25
LN_EPS = 1e-5
GN_EPS = 1e-6

W_A = N_HEADS_A * HEAD_DIM_A
W_IQ = N_HEADS_IDX * HEAD_DIM_IDX
W_RQK = N_HEADS_R * KEY_DIM_R
W_RV = N_HEADS_R * VAL_DIM_R
SPLIT_SIZES = (W_A, W_A, W_A, W_IQ, HEAD_DIM_IDX, N_HEADS_IDX, W_RQK, W_RQK, W_RV, W_RV, D_MODEL, D_MODEL)

LANES = 128
VMEM_LIMIT_BYTES = 60 * 1024 * 1024

OFF_QA = 0
OFF_KA = OFF_QA + W_A
OFF_VA = OFF_KA + W_A
OFF_QI = OFF_VA + W_A
OFF_KI4 = OFF_QI + W_IQ
OFF_WI = OFF_KI4 + W_IQ
OFF_QR = OFF_WI + LANES
OFF_KR = OFF_QR + W_RQK
OFF_VR = OFF_KR + W_RQK
OFF_GR = OFF_VR + W_RV
OFF_GA = OFF_GR + W_RV
OFF_GG = OFF_GA + D_MODEL
W_TOTAL = OFF_GG + D_MODEL

INT_MIN = -(2 ** 31)
NEG_BIG = -1e30
LOG2_E = math.log2(math.e)
FOLD_CHAINS = 8
BF16 = jnp.bfloat16
F32 = jnp.float32


def _dot(a, b):
    return jnp.dot(a, b, preferred_element_type=F32)


def _dot_nt(a, b):
    return lax.dot_general(a, b, (((1,), (1,)), ((), ())), preferred_element_type=F32)


def _dot_tn(a, b):
    return lax.dot_general(a, b, (((0,), (0,)), ((), ())), preferred_element_type=F32)


def _layer_norm(z, g, b):
    mu = jnp.mean(z, axis=-1, keepdims=True)
    d = z - mu
    var = jnp.mean(d * d, axis=-1, keepdims=True)
    return d * lax.rsqrt(var + LN_EPS) * g + b


def _const_spec(shape):
    nd = len(shape)
    return pl.BlockSpec(shape, lambda *_: (0,) * nd, pipeline_mode=pl.Buffered(1))


def _proj_kernel(x_ref, w_ref, cos_ref, sin_ref, g4_ref, b4_ref,
                 qa_o, kf_o, kb_o, vf_o, vb_o, qi_o, kif_o, ki4_o, wi_o,
                 qr_o, kr_o, vr_o, gr_o, ga_o, gg_o):
    xb = x_ref[...].astype(BF16)

    def seg(off, n):
        return _dot(xb, w_ref[:, off:off + n])

    qa_o[...] = (seg(OFF_QA, W_A) * (HEAD_DIM_A ** -0.5 * LOG2_E)).astype(BF16)
    k = seg(OFF_KA, W_A)
    kf_o[...] = k
    kb_o[...] = k.astype(BF16)
    v = seg(OFF_VA, W_A)
    vf_o[...] = v
    vb_o[...] = v.astype(BF16)
    qi_o[...] = seg(OFF_QI, W_IQ).astype(BF16)

    ki = seg(OFF_KI4, W_IQ)
    first = lax.broadcasted_iota(jnp.int32, ki.shape, 1) < HEAD_DIM_IDX
    inv_n = 1.0 / HEAD_DIM_IDX
    mu = jnp.sum(jnp.where(first, ki, 0.0), axis=-1, keepdims=True) * inv_n
    d = ki - mu
    var = jnp.sum(jnp.where(first, d * d, 0.0), axis=-1, keepdims=True) * inv_n
    kin = d * lax.rsqrt(var + LN_EPS) * g4_ref[...] + b4_ref[...]
    kif_o[...] = kin[:, :HEAD_DIM_IDX]
    ki4_o[...] = kin.astype(BF16)

    wi_o[...] = seg(OFF_WI, LANES) * ((N_HEADS_IDX ** -0.5) * (HEAD_DIM_IDX ** -0.5))

    for off, o_ref, scale in ((OFF_QR, qr_o, 1.0), (OFF_KR, kr_o, KEY_DIM_R ** -0.5)):
        h = seg(off, W_RQK)
        for j in range(W_RQK // LANES):
            xj = h[:, j * LANES:(j + 1) * LANES]
            rj = pltpu.roll(xj, LANES // 2, 1)
            oj = xj * cos_ref[...] + rj * sin_ref[...]
            o_ref[:, j * LANES:(j + 1) * LANES] = (oj * scale).astype(BF16)

    vr_o[...] = seg(OFF_VR, W_RV).astype(BF16)
    gr_o[...] = seg(OFF_GR, W_RV).astype(BF16)
    ga_o[...] = seg(OFF_GA, D_MODEL).astype(BF16)
    gg_o[...] = seg(OFF_GG, D_MODEL).astype(BF16)


def _proj(x2d, w_all, cos_t, sin_t, g4, b4, *, tm, n_pos_blocks):
    m = x2d.shape[0]
    grid = (m // tm,)

    def row(width, dtype):
        return jax.ShapeDtypeStruct((m, width), dtype)

    def rspec(width):
        return pl.BlockSpec((tm, width), lambda i: (i, 0))

    pos_spec = pl.BlockSpec((tm, LANES), lambda i: (i % n_pos_blocks, 0))
    out_shapes = (
        row(W_A, BF16), row(W_A, F32), row(W_A, BF16), row(W_A, F32), row(W_A, BF16),
        row(W_IQ, BF16), row(HEAD_DIM_IDX, F32), row(W_IQ, BF16), row(LANES, F32),
        row(W_RQK, BF16), row(W_RQK, BF16), row(W_RV, BF16), row(W_RV, BF16),
        row(D_MODEL, BF16), row(D_MODEL, BF16))
    out_specs = tuple(rspec(s.shape[1]) for s in out_shapes)
    return pl.pallas_call(
        _proj_kernel,
        grid=grid,
        in_specs=[rspec(D_MODEL), _const_spec((D_MODEL, W_TOTAL)), pos_spec, pos_spec,
                  _const_spec((1, W_IQ)), _const_spec((1, W_IQ))],
        out_specs=out_specs,
        out_shape=out_shapes,
        compiler_params=pltpu.CompilerParams(
            dimension_semantics=("arbitrary",), vmem_limit_bytes=VMEM_LIMIT_BYTES),
        name="proj",
    )(x2d, w_all, cos_t, sin_t, g4, b4)


def _attn_kernel(qa_ref, qi_ref, wi_ref, kb_ref, vt_ref, ki_ref, bias_ref, tri_ref, o_ref,
                 key_scr, qt_scr, m_scr, l_scr, acc_scr, *, qb, sl, n_keys, pos0, near_off, ktop):
    i = pl.program_id(1)
    t0 = pos0 + i * qb
    lblk = jnp.minimum(n_keys, ((t0 + qb - 1) // CHUNK + 1) * CHUNK)
    nslab = (lblk + sl - 1) // sl
    n_lane_tiles = qb // LANES

    def fold8(x, op=jnp.sum):
        parts = op(x.reshape(sl // (8 * FOLD_CHAINS), FOLD_CHAINS, 8, qb), axis=0)
        return op(parts, axis=0)

    qi_t = qi_ref[...].astype(F32).T
    qi_h = [qi_t[h * HEAD_DIM_IDX:(h + 1) * HEAD_DIM_IDX].astype(BF16) for h in range(N_HEADS_IDX)]
    w_t = wi_ref[...].T
    w_h = [w_t[HEAD_DIM_IDX + h:HEAD_DIM_IDX + h + 1] for h in range(N_HEADS_IDX)]
    qpos = t0 + lax.broadcasted_iota(jnp.int32, (1, qb), 1)
    qlim = jnp.minimum((lax.shift_right_logical(qpos, int(math.log2(CHUNK))) + 1) * CHUNK, n_keys)
    key_row = lax.broadcasted_iota(jnp.int32, (sl, qb), 0)

    def a_body(j, carry):
        s0 = pl.multiple_of(j * sl, sl)
        kis = ki_ref[pl.ds(s0, sl), 0:HEAD_DIM_IDX]
        s = w_h[0] * jnp.maximum(_dot(kis, qi_h[0]), 0.0)
        for h in range(1, N_HEADS_IDX):
            s = s + w_h[h] * jnp.maximum(_dot(kis, qi_h[h]), 0.0)
        b = lax.bitcast_convert_type(s, jnp.int32)
        skey = jnp.where(b < 0, INT_MIN - b, b)
        key_scr[j] = jnp.where(s0 + key_row < qlim, skey, INT_MIN)
        return carry

    lax.fori_loop(0, nslab, a_body, 0)

    def count_ge(cand_s):
        def c_body(j, acc):
            return acc + fold8(jnp.where(key_scr[j] >= cand_s, 1.0, 0.0))
        acc = lax.fori_loop(0, nslab, c_body, jnp.zeros((8, qb), F32))
        return jnp.sum(acc, axis=0, keepdims=True)

    def bit_body(it, t_u):
        cand_u = t_u | jnp.left_shift(jnp.int32(1), 31 - it)
        return jnp.where(count_ge(cand_u ^ INT_MIN) >= ktop, cand_u, t_u)

    t_u = lax.fori_loop(0, 32, bit_body, jnp.zeros((1, qb), jnp.int32))
    thr = t_u ^ INT_MIN
    int_max = 2 ** 31 - 1
    n_gt = jnp.where(thr == int_max, 0.0, count_ge(jnp.where(thr == int_max, thr, thr + 1)))
    need = jnp.where(thr == INT_MIN, 0.0, ktop - n_gt)

    def m_body(j, carry):
        keys = key_scr[j]
        eq = keys == thr
        pref = _dot(tri_ref[...], jnp.where(eq, 1.0, 0.0).astype(BF16))
        sel = (keys > thr) | (eq & (pref + carry <= need))
        key_scr[j] = lax.bitcast_convert_type(jnp.where(sel, 0.0, NEG_BIG), jnp.int32)
        return carry + pref[sl - 1:sl]

    lax.fori_loop(0, nslab, m_body, jnp.zeros((1, qb), F32))

    jfar = jnp.minimum(jnp.maximum(t0 - near_off + sl - 1, 0) // sl, nslab)
    n_pairs = N_HEADS_A // 2
    qa_t = qa_ref[...].astype(F32).T
    low = lax.broadcasted_iota(jnp.int32, (LANES, qb), 0) < HEAD_DIM_A
    for p in range(n_pairs):
        blk = qa_t[p * LANES:(p + 1) * LANES]
        qt_scr[p, :, 0:qb] = jnp.where(low, blk, 0.0).astype(BF16)
        qt_scr[p, :, qb:2 * qb] = jnp.where(low, 0.0, blk).astype(BF16)
    m_scr[...] = jnp.full(m_scr.shape, NEG_BIG, F32)
    l_scr[...] = jnp.zeros(l_scr.shape, F32)
    acc_scr[...] = jnp.zeros(acc_scr.shape, F32)

    def d_body(j, carry, near):
        s0 = pl.multiple_of(j * sl, sl)
        madd = lax.bitcast_convert_type(key_scr[j], F32)
        for p in range(n_pairs):
            k2 = kb_ref[pl.ds(s0, sl), p * LANES:(p + 1) * LANES]
            l2 = _dot(k2, qt_scr[p])
            for e in range(2):
                h = 2 * p + e
                logit = l2[:, e * qb:(e + 1) * qb] + madd
                if near:
                    c0 = near_off - t0 + s0 + qb - LANES
                    logit = logit + jnp.concatenate(
                        [bias_ref[h, pl.ds(pl.multiple_of(c0 - g * LANES, LANES), sl), :]
                         for g in range(n_lane_tiles)], axis=1)
                m_prev = m_scr[h]
                m_new = jnp.maximum(m_prev, jnp.max(fold8(logit, jnp.max), axis=0, keepdims=True))
                alpha = jnp.exp2(m_prev - m_new)
                pexp = jnp.exp2(logit - m_new)
                l_scr[h] = alpha * l_scr[h] + fold8(pexp)
                acc_scr[h] = alpha * acc_scr[h] + _dot(
                    vt_ref[j, h * HEAD_DIM_A:(h + 1) * HEAD_DIM_A, :], pexp.astype(BF16))
                m_scr[h] = m_new
        return carry

    lax.fori_loop(0, jfar, functools.partial(d_body, near=False), 0)
    lax.fori_loop(jfar, nslab, functools.partial(d_body, near=True), 0)

    for p in range(n_pairs):
        outs = [acc_scr[h] / jnp.sum(l_scr[h], axis=0, keepdims=True) for h in (2 * p, 2 * p + 1)]
        o_ref[:, p * LANES:(p + 1) * LANES] = jnp.concatenate(outs, axis=0).T.astype(o_ref.dtype)


def _t5_bucket_np(rel):
    nb = NUM_BUCKETS // 2
    max_exact = nb // 2
    ret = np.where(rel > 0, nb, 0)
    n = np.abs(rel)
    nf = np.maximum(n, max_exact).astype(np.float64)
    large = max_exact + (np.log(nf / max_exact) / math.log(MAX_DISTANCE / max_exact) * (nb - max_exact)).astype(np.int64)
    large = np.minimum(large, nb - 1)
    return ret + np.where(n < max_exact, n, large)


def _t5_bucket(rel):
    nb = NUM_BUCKETS // 2
    max_exact = nb // 2
    ret = jnp.where(rel > 0, nb, 0)
    n = jnp.abs(rel)
    nf = jnp.maximum(n, max_exact).astype(F32)
    large = max_exact + (jnp.log(nf / max_exact) / math.log(MAX_DISTANCE / max_exact) * (nb - max_exact)).astype(jnp.int32)
    large = jnp.minimum(large, nb - 1)
    return ret + jnp.where(n < max_exact, n, large)


def _attention(qa, qi, wi, kb, vt, ki4, t5_bias, *, qb, sl, n_keys, pos0):
    bsz, tq, _ = qa.shape
    lp = kb.shape[1]
    assert tq % qb == 0 and lp % sl == 0 and sl % LANES == 0 and qb % LANES == 0
    ktop = min(TOPK_MAX, n_keys // 4)
    nslab_max = lp // sl

    gran = math.gcd(pos0, sl) if tq == qb else math.gcd(math.gcd(pos0, qb), sl)
    assert gran % LANES == 0
    rel_all = np.arange(-(n_keys + qb), 0)
    far_bucket = _t5_bucket_np(np.array([-(n_keys + qb)]))[0]
    sat = rel_all[_t5_bucket_np(rel_all) != far_bucket]
    n_sat = int(-sat.min()) + 1 if sat.size else 1
    near_off = -(-(sl - 1 - gran + n_sat) // gran) * gran
    rel_min = -(qb - 1) - near_off
    rels = jnp.arange(rel_min, sl, dtype=jnp.int32)
    tab = (t5_bias[_t5_bucket(rels)] - t5_bias[far_bucket][None, :]) * LOG2_E
    n_rel = sl - rel_min
    n_rows = near_off + sl + qb - LANES
    assert n_rows + LANES - 1 == n_rel
    hank = jnp.tile(tab.T, (1, LANES + 1))[:, :LANES * (n_rel + 1)].reshape(N_HEADS_A, LANES, n_rel + 1)
    master = jnp.transpose(hank[:, ::-1, :n_rows], (0, 2, 1))

    tri = jnp.asarray(np.tril(np.ones((sl, sl), np.float32)), dtype=BF16)

    kern = functools.partial(_attn_kernel, qb=qb, sl=sl, n_keys=n_keys, pos0=pos0,
                             near_off=near_off, ktop=float(ktop))

    def qspec(width_):
        return pl.BlockSpec((None, qb, width_), lambda b, i: (b, i, 0))

    def kspec(*shape):
        nd = len(shape)
        return pl.BlockSpec((None,) + shape, lambda b, i: (b,) + (0,) * nd, pipeline_mode=pl.Buffered(1))

    return pl.pallas_call(
        kern,
        grid=(bsz, tq // qb),
        in_specs=[qspec(W_A), qspec(W_IQ), qspec(LANES), kspec(lp, W_A), kspec(nslab_max, W_A, sl),
                  kspec(lp, W_IQ), _const_spec(master.shape), _const_spec((sl, sl))],
        out_specs=qspec(W_A),
        out_shape=jax.ShapeDtypeStruct((bsz, tq, W_A), BF16),
        scratch_shapes=[pltpu.VMEM((nslab_max, sl, qb), jnp.int32),
                        pltpu.VMEM((N_HEADS_A // 2, LANES, 2 * qb), BF16),
                        pltpu.VMEM((N_HEADS_A, 1, qb), F32),
                        pltpu.VMEM((N_HEADS_A, 8, qb), F32),
                        pltpu.VMEM((N_HEADS_A, HEAD_DIM_A, qb), F32)],
        compiler_params=pltpu.CompilerParams(
            dimension_semantics=("arbitrary", "arbitrary"), vmem_limit_bytes=VMEM_LIMIT_BYTES),
        name="dsa_attention",
    )(qa, qi, wi, kb, vt, ki4, master, tri)


def _ret_kernel(q_ref, k_ref, v_ref, g_ref, gn_ref, s0_ref, dm_ref, qd_ref, kd_ref, gc_ref,
                r_ref, sfin_ref, s_scr, *, n_chunks):
    c = pl.program_id(1)

    @pl.when(c == 0)
    def _():
        s_scr[...] = s0_ref[...]

    for h in range(N_HEADS_R):
        p = h // 2
        q2 = q_ref[:, p * LANES:(p + 1) * LANES].astype(F32)
        k2 = k_ref[:, p * LANES:(p + 1) * LANES]
        vh = v_ref[:, h * VAL_DIM_R:(h + 1) * VAL_DIM_R]
        s_h = s_scr[h]
        qm = (q2 * qd_ref[h, 0]).astype(BF16)
        qdec = (q2 * qd_ref[h, 1]).astype(BF16)
        kdec = (k2.astype(F32) * kd_ref[h]).astype(BF16)
        inner = _dot_nt(qm, k2) * dm_ref[h]
        o = _dot(inner.astype(BF16), vh) + _dot(qdec, s_h.astype(BF16))
        s_scr[h] = gc_ref[h] * s_h + _dot_tn(kdec, vh)
        mu = jnp.mean(o, axis=-1, keepdims=True)
        d = o - mu
        var = jnp.mean(d * d, axis=-1, keepdims=True)
        on = d * lax.rsqrt(var + GN_EPS) * gn_ref[:, h * VAL_DIM_R:(h + 1) * VAL_DIM_R]
        gate = g_ref[:, h * VAL_DIM_R:(h + 1) * VAL_DIM_R].astype(F32)
        r_ref[:, h * VAL_DIM_R:(h + 1) * VAL_DIM_R] = (on * (gate * jax.nn.sigmoid(gate))).astype(r_ref.dtype)

    @pl.when(c == n_chunks - 1)
    def _():
        sfin_ref[...] = s_scr[...]


def _pair_lane_heads():
    return (np.arange(LANES) // (KEY_DIM_R // 2)) % 2


def _retention(qr, kr, vr, gr, gn_g, s_init, log_gamma, *, chunk):
    bsz, t, _ = qr.shape
    n_chunks = t // chunk
    n = jnp.arange(chunk, dtype=F32)
    lg = log_gamma.astype(F32)
    diff = n[:, None] - n[None, :]
    dmask = jnp.where(diff >= 0, jnp.exp(lg[:, None, None] * jnp.maximum(diff, 0.0)), 0.0)
    owner = jnp.asarray(_pair_lane_heads()[None, :] == (np.arange(N_HEADS_R) % 2)[:, None], F32)
    q_dec = jnp.exp(lg[:, None] * (n + 1.0))
    k_dec = jnp.exp(lg[:, None] * (chunk - 1.0 - n))
    qd = jnp.stack([jnp.broadcast_to(owner[:, None, :], (N_HEADS_R, chunk, LANES)),
                    owner[:, None, :] * q_dec[:, :, None]], axis=1)
    kd = owner[:, None, :] * k_dec[:, :, None]
    gc = jnp.broadcast_to(jnp.exp(lg * chunk)[:, None, None], (N_HEADS_R, 1, LANES))

    def tspec(width):
        return pl.BlockSpec((None, chunk, width), lambda b, c: (b, c, 0))

    sspec = pl.BlockSpec((None, N_HEADS_R, LANES, VAL_DIM_R), lambda b, c: (b, 0, 0, 0))
    return pl.pallas_call(
        functools.partial(_ret_kernel, n_chunks=n_chunks),
        grid=(bsz, n_chunks),
        in_specs=[tspec(W_RQK), tspec(W_RQK), tspec(W_RV), tspec(W_RV), _const_spec((1, W_RV)), sspec,
                  _const_spec(dmask.shape), _const_spec(qd.shape), _const_spec(kd.shape),
                  _const_spec(gc.shape)],
        out_specs=(tspec(W_RV), sspec),
        out_shape=(jax.ShapeDtypeStruct((bsz, t, W_RV), BF16),
                   jax.ShapeDtypeStruct((bsz, N_HEADS_R, LANES, VAL_DIM_R), F32)),
        scratch_shapes=[pltpu.VMEM((N_HEADS_R, LANES, VAL_DIM_R), F32)],
        compiler_params=pltpu.CompilerParams(
            dimension_semantics=("arbitrary", "arbitrary"), vmem_limit_bytes=VMEM_LIMIT_BYTES),
        name="retention",
    )(qr, kr, vr, gr, gn_g, s_init, dmask, qd, kd, gc)


def _finish_kernel(x_ref, a_ref, r_ref, ga_ref, gg_ref, wpa, wpr, wo, ln1g, ln1b, wg, wu, wd, ln2g, ln2b,
                   y_ref):
    x = x_ref[...]
    merged = (jax.nn.sigmoid(ga_ref[...].astype(F32)) * _dot(a_ref[...], wpa[...])
              + jax.nn.sigmoid(gg_ref[...].astype(F32)) * _dot(r_ref[...], wpr[...]))
    x1 = _layer_norm(ALPHA * x + _dot(merged.astype(BF16), wo[...]), ln1g[...], ln1b[...])
    x1b = x1.astype(BF16)
    gate = _dot(x1b, wg[...])
    hidden = gate * jax.nn.sigmoid(gate) * _dot(x1b, wu[...])
    y = _layer_norm(ALPHA * x1 + _dot(hidden.astype(BF16), wd[...]), ln2g[...], ln2b[...])
    y_ref[...] = y


def _finish(x2d, a, r, ga, gg, wpa, wpr, wo, ln1g, ln1b, wg, wu, wd, ln2g, ln2b, *, tm):
    m = x2d.shape[0]

    def rspec(width):
        return pl.BlockSpec((tm, width), lambda i: (i, 0))

    consts = (wpa, wpr, wo, ln1g, ln1b, wg, wu, wd, ln2g, ln2b)
    return pl.pallas_call(
        _finish_kernel,
        grid=(m // tm,),
        in_specs=[rspec(D_MODEL), rspec(W_A), rspec(W_RV), rspec(D_MODEL), rspec(D_MODEL)]
        + [_const_spec(c.shape) for c in consts],
        out_specs=rspec(D_MODEL),
        out_shape=jax.ShapeDtypeStruct((m, D_MODEL), F32),
        compiler_params=pltpu.CompilerParams(
            dimension_semantics=("arbitrary",), vmem_limit_bytes=VMEM_LIMIT_BYTES),
        name="finish",
    )(x2d, a, r, ga, gg, *consts)


def _relayout_w_in(w_in):
    offs = np.cumsum((0,) + SPLIT_SIZES)
    parts = [w_in[:, offs[i]:offs[i + 1]] for i in range(len(SPLIT_SIZES))]
    w_qa, w_ka, w_va, w_qi, w_ki, w_wi, w_qr, w_kr, w_vr, w_gr, w_ga, w_gg = parts
    half = KEY_DIM_R // 2
    perm = np.concatenate([
        np.arange(hh * KEY_DIM_R + part * half, hh * KEY_DIM_R + (part + 1) * half)
        for p in range(N_HEADS_R // 2) for part in (0, 1) for hh in (2 * p, 2 * p + 1)])
    w_wi_pad = jnp.pad(w_wi, ((0, 0), (HEAD_DIM_IDX, LANES - HEAD_DIM_IDX - N_HEADS_IDX)))
    w_all = jnp.concatenate(
        [w_qa, w_ka, w_va, w_qi, jnp.tile(w_ki, (1, N_HEADS_IDX)), w_wi_pad,
         w_qr[:, perm], w_kr[:, perm], w_vr, w_gr, w_ga, w_gg], axis=1)
    assert w_all.shape[1] == W_TOTAL
    return w_all.astype(BF16)


def _rotary_tables(pos):
    half = KEY_DIM_R // 2
    inv_freq = ROPE_BASE ** (-jnp.arange(half, dtype=F32) / half)
    ang = pos.astype(F32)[:, None] * inv_freq[None, :]
    cos, sin = jnp.cos(ang), jnp.sin(ang)
    return jnp.tile(cos, (1, 4)), jnp.concatenate([-sin, -sin, sin, sin], axis=1)


def _group(x, pos0, past_k, past_v, past_ki, state, w_all, g4, b4, t5_bias, log_gamma, fin_w,
           *, tm_proj, qb, sl, ret_chunk, tm_fin):
    bsz, t, _ = x.shape
    m = bsz * t
    x2d = x.reshape(m, D_MODEL)
    pos = pos0 + jnp.arange(t, dtype=jnp.int32)
    cos_t, sin_t = _rotary_tables(pos)
    n_pos_blocks = max(t // tm_proj, 1)
    if tm_proj > t:
        cos_t = jnp.tile(cos_t, (tm_proj // t, 1))
        sin_t = jnp.tile(sin_t, (tm_proj // t, 1))
    (qa, kf, kb, vf, vb, qi, kif, ki4, wi, qr, kr, vr, gr, ga, gg) = _proj(
        x2d, w_all, cos_t, sin_t, g4, b4, tm=tm_proj, n_pos_blocks=n_pos_blocks)

    def b3(a):
        return a.reshape(bsz, t, a.shape[-1])

    n_keys = t if past_k is None else past_k.shape[1] + t
    lp = -(-n_keys // sl) * sl
    kb3, vb3, ki43 = b3(kb), b3(vb), b3(ki4)
    if past_k is not None:
        pk = past_k.reshape(bsz, -1, W_A).astype(BF16)
        pv = past_v.reshape(bsz, -1, W_A).astype(BF16)
        pki = jnp.tile(past_ki, (1, 1, N_HEADS_IDX)).astype(BF16)
        kb3 = jnp.concatenate([pk, kb3], axis=1)
        vb3 = jnp.concatenate([pv, vb3], axis=1)
        ki43 = jnp.concatenate([pki, ki43], axis=1)
    if lp != n_keys:
        padw = ((0, 0), (0, lp - n_keys), (0, 0))
        kb3, vb3, ki43 = jnp.pad(kb3, padw), jnp.pad(vb3, padw), jnp.pad(ki43, padw)
    vt = vb3.reshape(bsz, lp // sl, sl, W_A).transpose(0, 1, 3, 2)
    qa3, qi3, wi3 = b3(qa), b3(qi), b3(wi)
    tq = -(-t // qb) * qb
    if tq != t:
        padq = ((0, 0), (0, tq - t), (0, 0))
        qa3, qi3, wi3 = jnp.pad(qa3, padq), jnp.pad(qi3, padq), jnp.pad(wi3, padq)
    a = _attention(qa3, qi3, wi3, kb3, vt, ki43, t5_bias, qb=qb, sl=sl, n_keys=n_keys, pos0=pos0)[:, :t]

    half = KEY_DIM_R // 2
    npair = N_HEADS_R // 2
    if state is None:
        s_init = jnp.zeros((bsz, N_HEADS_R, LANES, VAL_DIM_R), F32)
    else:
        st = state.astype(F32).reshape(bsz, npair, 2, 2, half, VAL_DIM_R)
        zero = jnp.zeros_like(st[:, :, 0])
        s_init = jnp.stack([jnp.stack([st[:, :, 0], zero], axis=3),
                            jnp.stack([zero, st[:, :, 1]], axis=3)], axis=2)
        s_init = s_init.reshape(bsz, N_HEADS_R, LANES, VAL_DIM_R)
    r, s_fin = _retention(b3(qr), b3(kr), b3(vr), b3(gr), fin_w["gn"], s_init, log_gamma, chunk=ret_chunk)
    s6 = s_fin.reshape(bsz, npair, 2, 2, 2, half, VAL_DIM_R)
    s_out = jnp.stack([s6[:, :, 0, :, 0], s6[:, :, 1, :, 1]], axis=2)
    s_out = s_out.reshape(bsz, N_HEADS_R, KEY_DIM_R, VAL_DIM_R)

    y = _finish(x2d, a.reshape(m, W_A), r.reshape(m, W_RV), ga, gg,
                fin_w["wpa"], fin_w["wpr"], fin_w["wo"], fin_w["ln1g"], fin_w["ln1b"],
                fin_w["wg"], fin_w["wu"], fin_w["wd"], fin_w["ln2g"], fin_w["ln2b"], tm=tm_fin)
    return (y.reshape(bsz, t, D_MODEL),
            kf.reshape(bsz, t, N_HEADS_A, HEAD_DIM_A), vf.reshape(bsz, t, N_HEADS_A, HEAD_DIM_A),
            kif.reshape(bsz, t, HEAD_DIM_IDX), s_out)


def kernel(x_prompt, x_sample, cache_k, cache_v, cache_idx_k, state_ret, w_in, idx_k_norm_g, idx_k_norm_b,
           t5_bias, ret_gn_g, w_pa, w_pr, w_o, ln1_g, ln1_b, w_gate, w_up, w_down, ln2_g, ln2_b):
    assert w_in.shape[0] == DEPTH
    log_gamma = jnp.log1p(-jnp.exp2(-5.0 - jnp.arange(N_HEADS_R, dtype=F32)))
    l = 0
    w_all = _relayout_w_in(w_in[l])
    g4 = jnp.tile(idx_k_norm_g[l], N_HEADS_IDX)[None, :].astype(F32)
    b4 = jnp.tile(idx_k_norm_b[l], N_HEADS_IDX)[None, :].astype(F32)
    fin_w = dict(
        gn=ret_gn_g[l][None, :].astype(F32),
        wpa=w_pa[l].astype(BF16), wpr=w_pr[l].astype(BF16), wo=w_o[l].astype(BF16),
        ln1g=ln1_g[l][None, :].astype(F32), ln1b=ln1_b[l][None, :].astype(F32),
        wg=w_gate[l].astype(BF16), wu=w_up[l].astype(BF16), wd=w_down[l].astype(BF16),
        ln2g=ln2_g[l][None, :].astype(F32), ln2b=ln2_b[l][None, :].astype(F32))

    t_p = x_prompt.shape[1]
    yp, kp, vp, ikp, sp = _group(
        x_prompt, 0, None, None, None, None, w_all, g4, b4, t5_bias, log_gamma, fin_w,
        tm_proj=min(512, t_p), qb=min(256, t_p), sl=512, ret_chunk=min(256, t_p), tm_fin=min(256, t_p))

    bs, t_s, _ = x_sample.shape
    past = cache_k.shape[2]
    ys, ks, vs, iks, ss = _group(
        x_sample, past, cache_k[l], cache_v[l], cache_idx_k[l], state_ret[l], w_all, g4, b4, t5_bias,
        log_gamma, fin_w, tm_proj=bs * t_s, qb=LANES, sl=512, ret_chunk=t_s, tm_fin=bs * t_s)

    return (yp, ys, kp[None], vp[None], ikp[None], sp[None],
            ks[None], vs[None], iks[None], ss.astype(state_ret.dtype)[None])
```

```python
import functools
import math

import numpy as np
import jax
import jax.numpy as jnp
from jax import lax
from jax.experimental import pallas as pl
from jax.experimental.pallas import tpu as pltpu

D_MODEL = 1024
CHUNK = 64
N_HEADS_A = 8
HEAD_DIM_A = 64
N_HEADS_IDX = 4
HEAD_DIM_IDX = 64
TOPK_MAX = 256
NUM_BUCKETS = 32
MAX_DISTANCE = 1024
N_HEADS_R = 8
KEY_DIM_R = 64
VAL_DIM_R = 128
ROPE_BASE = 10000.0
D_FF = 2816
DEPTH = 1
ALPHA = (2.0 * DEPTH) ** 0.25
LN_EPS = 1e-5
GN_EPS = 1e-6

W_A = N_HEADS_A * HEAD_DIM_A
W_IQ = N_HEADS_IDX * HEAD_DIM_IDX
W_RQK = N_HEADS_R * KEY_DIM_R
W_RV = N_HEADS_R * VAL_DIM_R
SPLIT_SIZES = (W_A, W_A, W_A, W_IQ, HEAD_DIM_IDX, N_HEADS_IDX, W_RQK, W_RQK, W_RV, W_RV, D_MODEL, D_MODEL)

LANES = 128
VMEM_LIMIT_BYTES = 60 * 1024 * 1024

OFF_QA = 0
OFF_KA = OFF_QA + W_A
OFF_VA = OFF_KA + W_A
OFF_QI = OFF_VA + W_A
OFF_KI4 = OFF_QI + W_IQ
OFF_WI = OFF_KI4 + W_IQ
OFF_QR = OFF_WI + LANES
OFF_KR = OFF_QR + W_RQK
OFF_VR = OFF_KR + W_RQK
OFF_GR = OFF_VR + W_RV
OFF_GA = OFF_GR + W_RV
OFF_GG = OFF_GA + D_MODEL
W_TOTAL = OFF_GG + D_MODEL

INT_MIN = -(2 ** 31)
NEG_BIG = -1e30
LOG2_E = math.log2(math.e)
FOLD_CHAINS = 8
BF16 = jnp.bfloat16
F32 = jnp.float32
I16 = jnp.int16
ONE16, ZERO16 = np.int16(1), np.int16(0)
MAX16, MIN16 = np.int16(2 ** 15 - 1), np.int16(-(2 ** 15))


def _dot(a, b):
    return jnp.dot(a, b, preferred_element_type=F32)


def _dot_nt(a, b):
    return lax.dot_general(a, b, (((1,), (1,)), ((), ())), preferred_element_type=F32)


def _dot_tn(a, b):
    return lax.dot_general(a, b, (((0,), (0,)), ((), ())), preferred_element_type=F32)


def _layer_norm(z, g, b):
    mu = jnp.mean(z, axis=-1, keepdims=True)
    d = z - mu
    var = jnp.mean(d * d, axis=-1, keepdims=True)
    return d * lax.rsqrt(var + LN_EPS) * g + b


def _const_spec(shape):
    nd = len(shape)
    return pl.BlockSpec(shape, lambda *_: (0,) * nd, pipeline_mode=pl.Buffered(1))


def _proj_kernel(x_ref, w_ref, cos_ref, sin_ref, g4_ref, b4_ref,
                 qa_o, kf_o, kb_o, vf_o, vb_o, qi_o, kif_o, ki4_o, wi_o,
                 qr_o, kr_o, vr_o, gr_o, ga_o, gg_o):
    xb = x_ref[...].astype(BF16)

    def seg(off, n):
        return _dot(xb, w_ref[:, off:off + n])

    qa_o[...] = (seg(OFF_QA, W_A) * (HEAD_DIM_A ** -0.5 * LOG2_E)).astype(BF16)
    k = seg(OFF_KA, W_A)
    kf_o[...] = k
    kb_o[...] = k.astype(BF16)
    v = seg(OFF_VA, W_A)
    vf_o[...] = v
    vb_o[...] = v.astype(BF16)
    qi_o[...] = seg(OFF_QI, W_IQ).astype(BF16)

    ki = seg(OFF_KI4, W_IQ)
    first = lax.broadcasted_iota(jnp.int32, ki.shape, 1) < HEAD_DIM_IDX
    inv_n = 1.0 / HEAD_DIM_IDX
    mu = jnp.sum(jnp.where(first, ki, 0.0), axis=-1, keepdims=True) * inv_n
    d = ki - mu
    var = jnp.sum(jnp.where(first, d * d, 0.0), axis=-1, keepdims=True) * inv_n
    kin = d * lax.rsqrt(var + LN_EPS) * g4_ref[...] + b4_ref[...]
    kif_o[...] = kin[:, :HEAD_DIM_IDX]
    ki4_o[...] = kin.astype(BF16)

    wi_o[...] = seg(OFF_WI, LANES) * ((N_HEADS_IDX ** -0.5) * (HEAD_DIM_IDX ** -0.5))

    for off, o_ref, scale in ((OFF_QR, qr_o, 1.0), (OFF_KR, kr_o, KEY_DIM_R ** -0.5)):
        h = seg(off, W_RQK)
        for j in range(W_RQK // LANES):
            xj = h[:, j * LANES:(j + 1) * LANES]
            rj = pltpu.roll(xj, LANES // 2, 1)
            oj = xj * cos_ref[...] + rj * sin_ref[...]
            o_ref[:, j * LANES:(j + 1) * LANES] = (oj * scale).astype(BF16)

    vr_o[...] = seg(OFF_VR, W_RV).astype(BF16)
    gr_o[...] = seg(OFF_GR, W_RV).astype(BF16)
    ga_o[...] = seg(OFF_GA, D_MODEL).astype(BF16)
    gg_o[...] = seg(OFF_GG, D_MODEL).astype(BF16)


def _proj(x2d, w_all, cos_t, sin_t, g4, b4, *, tm, n_pos_blocks):
    m = x2d.shape[0]
    grid = (m // tm,)

    def row(width, dtype):
        return jax.ShapeDtypeStruct((m, width), dtype)

    def rspec(width):
        return pl.BlockSpec((tm, width), lambda i: (i, 0))

    pos_spec = pl.BlockSpec((tm, LANES), lambda i: (i % n_pos_blocks, 0))
    out_shapes = (
        row(W_A, BF16), row(W_A, F32), row(W_A, BF16), row(W_A, F32), row(W_A, BF16),
        row(W_IQ, BF16), row(HEAD_DIM_IDX, F32), row(W_IQ, BF16), row(LANES, F32),
        row(W_RQK, BF16), row(W_RQK, BF16), row(W_RV, BF16), row(W_RV, BF16),
        row(D_MODEL, BF16), row(D_MODEL, BF16))
    out_specs = tuple(rspec(s.shape[1]) for s in out_shapes)
    return pl.pallas_call(
        _proj_kernel,
        grid=grid,
        in_specs=[rspec(D_MODEL), _const_spec((D_MODEL, W_TOTAL)), pos_spec, pos_spec,
                  _const_spec((1, W_IQ)), _const_spec((1, W_IQ))],
        out_specs=out_specs,
        out_shape=out_shapes,
        compiler_params=pltpu.CompilerParams(
            dimension_semantics=("arbitrary",), vmem_limit_bytes=VMEM_LIMIT_BYTES),
        name="proj",
    )(x2d, w_all, cos_t, sin_t, g4, b4)


def _attn_kernel(qa_ref, qi_ref, wi_ref, kb_ref, vt_ref, ki_ref, bias_ref, tri_ref, o_ref,
                 key_scr, half_scr, qt_scr, m_scr, l_scr, acc_scr, lg_scr, cm_scr,
                 *, qb, sl, n_keys, pos0, near_off, ktop):
    i = pl.program_id(1)
    t0 = pos0 + i * qb
    lblk = jnp.minimum(n_keys, ((t0 + qb - 1) // CHUNK + 1) * CHUNK)
    nslab = (lblk + sl - 1) // sl
    n_lane_tiles = qb // LANES

    def fold8(x, op=jnp.sum):
        parts = op(x.reshape(sl // (8 * FOLD_CHAINS), FOLD_CHAINS, 8, qb), axis=0)
        return op(parts, axis=0)

    qi_t = qi_ref[...].astype(F32).T
    qi_h = [qi_t[h * HEAD_DIM_IDX:(h + 1) * HEAD_DIM_IDX].astype(BF16) for h in range(N_HEADS_IDX)]
    w_t = wi_ref[...].T
    w_h = [w_t[HEAD_DIM_IDX + h:HEAD_DIM_IDX + h + 1] for h in range(N_HEADS_IDX)]
    qpos = t0 + lax.broadcasted_iota(jnp.int32, (1, qb), 1)
    qlim = jnp.minimum((lax.shift_right_logical(qpos, int(math.log2(CHUNK))) + 1) * CHUNK, n_keys)
    key_row = lax.broadcasted_iota(jnp.int32, (sl, qb), 0)

    def a_body(j, carry):
        s0 = pl.multiple_of(j * sl, sl)
        kis = ki_ref[pl.ds(s0, sl), 0:HEAD_DIM_IDX]
        s = w_h[0] * jnp.maximum(_dot(kis, qi_h[0]), 0.0)
        for h in range(1, N_HEADS_IDX):
            s = s + w_h[h] * jnp.maximum(_dot(kis, qi_h[h]), 0.0)
        b = lax.bitcast_convert_type(s, jnp.int32)
        skey = jnp.where(b < 0, INT_MIN - b, b)
        skey = jnp.where(s0 + key_row < qlim, skey, INT_MIN)
        key_scr[j] = skey
        half_scr[j] = lax.shift_right_arithmetic(skey, 16).astype(I16)
        return carry

    lax.fori_loop(0, nslab, a_body, 0)

    def count_ge(cand_s):
        def c_body(j, acc):
            return acc + fold8(jnp.where(key_scr[j] >= cand_s, 1.0, 0.0))
        acc = lax.fori_loop(0, nslab, c_body, jnp.zeros((8, qb), F32))
        return jnp.sum(acc, axis=0, keepdims=True)

    def count_ge16(cand16):
        def c_body(j, acc):
            hit = jnp.where(half_scr[j] >= cand16, ONE16, ZERO16)
            hit = hit.reshape(sl // (16 * FOLD_CHAINS), FOLD_CHAINS, 16, qb)
            for a in range(hit.shape[0]):
                acc = acc + hit[a]
            return acc
        acc = lax.fori_loop(0, nslab, c_body, jnp.zeros((FOLD_CHAINS, 16, qb), I16))
        return jnp.sum(acc.astype(F32).reshape(FOLD_CHAINS * 16, qb), axis=0, keepdims=True)

    def radix16():
        def bit_body(it, t_u):
            cand_u = t_u | jnp.left_shift(jnp.int32(1), 15 - it)
            cnt = count_ge16((cand_u - 2 ** 15).astype(I16))
            return jnp.where(cnt >= ktop, cand_u, t_u)
        return lax.fori_loop(0, 16, bit_body, jnp.zeros((1, qb), jnp.int32))

    hi_u = radix16()
    hi16 = (hi_u - 2 ** 15).astype(I16)

    def low_body(j, carry):
        low = ((key_scr[j] & 0xFFFF) - 2 ** 15).astype(I16)
        high = half_scr[j]
        half_scr[j] = jnp.where(high == hi16, low, jnp.where(high > hi16, MAX16, MIN16))
        return carry

    lax.fori_loop(0, nslab, low_body, 0)
    thr = (hi_u - 2 ** 15) * 2 ** 16 + radix16()
    int_max = 2 ** 31 - 1
    n_gt = jnp.where(thr == int_max, 0.0, count_ge(jnp.where(thr == int_max, thr, thr + 1)))
    need = jnp.where(thr == INT_MIN, 0.0, ktop - n_gt)

    def m_body(j, carry):
        keys = key_scr[j]
        eq = keys == thr
        pref = _dot(tri_ref[...], jnp.where(eq, 1.0, 0.0).astype(BF16))
        sel = (keys > thr) | (eq & (pref + carry <= need))
        key_scr[j] = lax.bitcast_convert_type(jnp.where(sel, 0.0, NEG_BIG), jnp.int32)
        return carry + pref[sl - 1:sl]

    lax.fori_loop(0, nslab, m_body, jnp.zeros((1, qb), F32))

    jfar = jnp.minimum(jnp.maximum(t0 - near_off + sl - 1, 0) // sl, nslab)
    n_pairs = N_HEADS_A // 2
    qa_t = qa_ref[...].astype(F32).T
    low = lax.broadcasted_iota(jnp.int32, (LANES, qb), 0) < HEAD_DIM_A
    for p in range(n_pairs):
        blk = qa_t[p * LANES:(p + 1) * LANES]
        qt_scr[p, :, 0:qb] = jnp.where(low, blk, 0.0).astype(BF16)
        qt_scr[p, :, qb:2 * qb] = jnp.where(low, 0.0, blk).astype(BF16)
    m_scr[...] = jnp.full(m_scr.shape, NEG_BIG, F32)
    l_scr[...] = jnp.zeros(l_scr.shape, F32)
    acc_scr[...] = jnp.zeros(acc_scr.shape, F32)

    def logits_stage(j, buf, near):
        s0 = pl.multiple_of(j * sl, sl)
        madd = lax.bitcast_convert_type(key_scr[j], F32)
        for p in range(n_pairs):
            k2 = kb_ref[pl.ds(s0, sl), p * LANES:(p + 1) * LANES]
            l2 = _dot(k2, qt_scr[p])
            for e in range(2):
                h = 2 * p + e
                logit = l2[:, e * qb:(e + 1) * qb] + madd
                if near:
                    c0 = near_off - t0 + s0 + qb - LANES
                    logit = logit + jnp.concatenate(
                        [bias_ref[h, pl.ds(pl.multiple_of(c0 - g * LANES, LANES), sl), :]
                         for g in range(n_lane_tiles)], axis=1)
                lg_scr[buf, h] = logit
                cm_scr[buf, h] = jnp.max(fold8(logit, jnp.max), axis=0, keepdims=True)

    def softmax_stage(j, buf):
        for h in range(N_HEADS_A):
            m_prev = m_scr[h]
            m_new = jnp.maximum(m_prev, cm_scr[buf, h])
            alpha = jnp.exp2(m_prev - m_new)
            pexp = jnp.exp2(lg_scr[buf, h] - m_new)
            l_scr[h] = alpha * l_scr[h] + fold8(pexp)
            acc_scr[h] = alpha * acc_scr[h] + _dot(
                vt_ref[j, h * HEAD_DIM_A:(h + 1) * HEAD_DIM_A, :], pexp.astype(BF16))
            m_scr[h] = m_new

    def d_body(j, carry, near_next):
        buf = lax.rem(j, 2)
        softmax_stage(j, buf)
        logits_stage(j + 1, 1 - buf, near_next)
        return carry

    lax.cond(jfar > 0, lambda: logits_stage(0, 0, False), lambda: logits_stage(0, 0, True))
    split = jnp.maximum(jfar - 1, 0)
    lax.fori_loop(0, split, functools.partial(d_body, near_next=False), 0)
    lax.fori_loop(split, nslab - 1, functools.partial(d_body, near_next=True), 0)
    softmax_stage(nslab - 1, lax.rem(nslab - 1, 2))

    for p in range(n_pairs):
        outs = [acc_scr[h] / jnp.sum(l_scr[h], axis=0, keepdims=True) for h in (2 * p, 2 * p + 1)]
        o_ref[:, p * LANES:(p + 1) * LANES] = jnp.concatenate(outs, axis=0).T.astype(o_ref.dtype)


def _t5_bucket_np(rel):
    nb = NUM_BUCKETS // 2
    max_exact = nb // 2
    ret = np.where(rel > 0, nb, 0)
    n = np.abs(rel)
    nf = np.maximum(n, max_exact).astype(np.float64)
    large = max_exact + (np.log(nf / max_exact) / math.log(MAX_DISTANCE / max_exact) * (nb - max_exact)).astype(np.int64)
    large = np.minimum(large, nb - 1)
    return ret + np.where(n < max_exact, n, large)


def _t5_bucket(rel):
    nb = NUM_BUCKETS // 2
    max_exact = nb // 2
    ret = jnp.where(rel > 0, nb, 0)
    n = jnp.abs(rel)
    nf = jnp.maximum(n, max_exact).astype(F32)
    large = max_exact + (jnp.log(nf / max_exact) / math.log(MAX_DISTANCE / max_exact) * (nb - max_exact)).astype(jnp.int32)
    large = jnp.minimum(large, nb - 1)
    return ret + jnp.where(n < max_exact, n, large)


def _attention(qa, qi, wi, kb, vt, ki4, t5_bias, *, qb, sl, n_keys, pos0):
    bsz, tq, _ = qa.shape
    lp = kb.shape[1]
    assert tq % qb == 0 and lp % sl == 0 and sl % LANES == 0 and qb % LANES == 0
    ktop = min(TOPK_MAX, n_keys // 4)
    nslab_max = lp // sl

    gran = math.gcd(pos0, sl) if tq == qb else math.gcd(math.gcd(pos0, qb), sl)
    assert gran % LANES == 0
    rel_all = np.arange(-(n_keys + qb), 0)
    far_bucket = _t5_bucket_np(np.array([-(n_keys + qb)]))[0]
    sat = rel_all[_t5_bucket_np(rel_all) != far_bucket]
    n_sat = int(-sat.min()) + 1 if sat.size else 1
    near_off = -(-(sl - 1 - gran + n_sat) // gran) * gran
    rel_min = -(qb - 1) - near_off
    rels = jnp.arange(rel_min, sl, dtype=jnp.int32)
    tab = (t5_bias[_t5_bucket(rels)] - t5_bias[far_bucket][None, :]) * LOG2_E
    n_rel = sl - rel_min
    n_rows = near_off + sl + qb - LANES
    assert n_rows + LANES - 1 == n_rel
    hank = jnp.tile(tab.T, (1, LANES + 1))[:, :LANES * (n_rel + 1)].reshape(N_HEADS_A, LANES, n_rel + 1)
    master = jnp.transpose(hank[:, ::-1, :n_rows], (0, 2, 1))

    tri = jnp.asarray(np.tril(np.ones((sl, sl), np.float32)), dtype=BF16)

    kern = functools.partial(_attn_kernel, qb=qb, sl=sl, n_keys=n_keys, pos0=pos0,
                             near_off=near_off, ktop=float(ktop))

    def qspec(width_):
        return pl.BlockSpec((None, qb, width_), lambda b, i: (b, i, 0))

    def kspec(*shape):
        nd = len(shape)
        return pl.BlockSpec((None,) + shape, lambda b, i: (b,) + (0,) * nd, pipeline_mode=pl.Buffered(1))

    return pl.pallas_call(
        kern,
        grid=(bsz, tq // qb),
        in_specs=[qspec(W_A), qspec(W_IQ), qspec(LANES), kspec(lp, W_A), kspec(nslab_max, W_A, sl),
                  kspec(lp, W_IQ), _const_spec(master.shape), _const_spec((sl, sl))],
        out_specs=qspec(W_A),
        out_shape=jax.ShapeDtypeStruct((bsz, tq, W_A), BF16),
        scratch_shapes=[pltpu.VMEM((nslab_max, sl, qb), jnp.int32),
                        pltpu.VMEM((nslab_max, sl, qb), I16),
                        pltpu.VMEM((N_HEADS_A // 2, LANES, 2 * qb), BF16),
                        pltpu.VMEM((N_HEADS_A, 1, qb), F32),
                        pltpu.VMEM((N_HEADS_A, 8, qb), F32),
                        pltpu.VMEM((N_HEADS_A, HEAD_DIM_A, qb), F32),
                        pltpu.VMEM((2, N_HEADS_A, sl, qb), F32),
                        pltpu.VMEM((2, N_HEADS_A, 1, qb), F32)],
        compiler_params=pltpu.CompilerParams(
            dimension_semantics=("arbitrary", "arbitrary"), vmem_limit_bytes=VMEM_LIMIT_BYTES),
        name="dsa_attention",
    )(qa, qi, wi, kb, vt, ki4, master, tri)


def _ret_kernel(q_ref, k_ref, v_ref, g_ref, gn_ref, s0_ref, dm_ref, qd_ref, kd_ref, gc_ref,
                r_ref, sfin_ref, s_scr, *, n_chunks):
    c = pl.program_id(1)

    @pl.when(c == 0)
    def _():
        s_scr[...] = s0_ref[...]

    for h in range(N_HEADS_R):
        p = h // 2
        q2 = q_ref[:, p * LANES:(p + 1) * LANES].astype(F32)
        k2 = k_ref[:, p * LANES:(p + 1) * LANES]
        vh = v_ref[:, h * VAL_DIM_R:(h + 1) * VAL_DIM_R]
        s_h = s_scr[h]
        qm = (q2 * qd_ref[h, 0]).astype(BF16)
        qdec = (q2 * qd_ref[h, 1]).astype(BF16)
        kdec = (k2.astype(F32) * kd_ref[h]).astype(BF16)
        inner = _dot_nt(qm, k2) * dm_ref[h]
        o = _dot(inner.astype(BF16), vh) + _dot(qdec, s_h.astype(BF16))
        s_scr[h] = gc_ref[h] * s_h + _dot_tn(kdec, vh)
        mu = jnp.mean(o, axis=-1, keepdims=True)
        d = o - mu
        var = jnp.mean(d * d, axis=-1, keepdims=True)
        on = d * lax.rsqrt(var + GN_EPS) * gn_ref[:, h * VAL_DIM_R:(h + 1) * VAL_DIM_R]
        gate = g_ref[:, h * VAL_DIM_R:(h + 1) * VAL_DIM_R].astype(F32)
        r_ref[:, h * VAL_DIM_R:(h + 1) * VAL_DIM_R] = (on * (gate * jax.nn.sigmoid(gate))).astype(r_ref.dtype)

    @pl.when(c == n_chunks - 1)
    def _():
        sfin_ref[...] = s_scr[...]


def _pair_lane_heads():
    return (np.arange(LANES) // (KEY_DIM_R // 2)) % 2


def _retention(qr, kr, vr, gr, gn_g, s_init, log_gamma, *, chunk):
    bsz, t, _ = qr.shape
    n_chunks = t // chunk
    n = jnp.arange(chunk, dtype=F32)
    lg = log_gamma.astype(F32)
    diff = n[:, None] - n[None, :]
    dmask = jnp.where(diff >= 0, jnp.exp(lg[:, None, None] * jnp.maximum(diff, 0.0)), 0.0)
    owner = jnp.asarray(_pair_lane_heads()[None, :] == (np.arange(N_HEADS_R) % 2)[:, None], F32)
    q_dec = jnp.exp(lg[:, None] * (n + 1.0))
    k_dec = jnp.exp(lg[:, None] * (chunk - 1.0 - n))
    qd = jnp.stack([jnp.broadcast_to(owner[:, None, :], (N_HEADS_R, chunk, LANES)),
                    owner[:, None, :] * q_dec[:, :, None]], axis=1)
    kd = owner[:, None, :] * k_dec[:, :, None]
    gc = jnp.broadcast_to(jnp.exp(lg * chunk)[:, None, None], (N_HEADS_R, 1, LANES))

    def tspec(width):
        return pl.BlockSpec((None, chunk, width), lambda b, c: (b, c, 0))

    sspec = pl.BlockSpec((None, N_HEADS_R, LANES, VAL_DIM_R), lambda b, c: (b, 0, 0, 0))
    return pl.pallas_call(
        functools.partial(_ret_kernel, n_chunks=n_chunks),
        grid=(bsz, n_chunks),
        in_specs=[tspec(W_RQK), tspec(W_RQK), tspec(W_RV), tspec(W_RV), _const_spec((1, W_RV)), sspec,
                  _const_spec(dmask.shape), _const_spec(qd.shape), _const_spec(kd.shape),
                  _const_spec(gc.shape)],
        out_specs=(tspec(W_RV), sspec),
        out_shape=(jax.ShapeDtypeStruct((bsz, t, W_RV), BF16),
                   jax.ShapeDtypeStruct((bsz, N_HEADS_R, LANES, VAL_DIM_R), F32)),
        scratch_shapes=[pltpu.VMEM((N_HEADS_R, LANES, VAL_DIM_R), F32)],
        compiler_params=pltpu.CompilerParams(
            dimension_semantics=("arbitrary", "arbitrary"), vmem_limit_bytes=VMEM_LIMIT_BYTES),
        name="retention",
    )(qr, kr, vr, gr, gn_g, s_init, dmask, qd, kd, gc)


def _finish_kernel(x_ref, a_ref, r_ref, ga_ref, gg_ref, wpa, wpr, wo, ln1g, ln1b, wg, wu, wd, ln2g, ln2b,
                   y_ref):
    x = x_ref[...]
    merged = (jax.nn.sigmoid(ga_ref[...].astype(F32)) * _dot(a_ref[...], wpa[...])
              + jax.nn.sigmoid(gg_ref[...].astype(F32)) * _dot(r_ref[...], wpr[...]))
    x1 = _layer_norm(ALPHA * x + _dot(merged.astype(BF16), wo[...]), ln1g[...], ln1b[...])
    x1b = x1.astype(BF16)
    gate = _dot(x1b, wg[...])
    hidden = gate * jax.nn.sigmoid(gate) * _dot(x1b, wu[...])
    y = _layer_norm(ALPHA * x1 + _dot(hidden.astype(BF16), wd[...]), ln2g[...], ln2b[...])
    y_ref[...] = y


def _finish(x2d, a, r, ga, gg, wpa, wpr, wo, ln1g, ln1b, wg, wu, wd, ln2g, ln2b, *, tm):
    m = x2d.shape[0]

    def rspec(width):
        return pl.BlockSpec((tm, width), lambda i: (i, 0))

    consts = (wpa, wpr, wo, ln1g, ln1b, wg, wu, wd, ln2g, ln2b)
    return pl.pallas_call(
        _finish_kernel,
        grid=(m // tm,),
        in_specs=[rspec(D_MODEL), rspec(W_A), rspec(W_RV), rspec(D_MODEL), rspec(D_MODEL)]
        + [_const_spec(c.shape) for c in consts],
        out_specs=rspec(D_MODEL),
        out_shape=jax.ShapeDtypeStruct((m, D_MODEL), F32),
        compiler_params=pltpu.CompilerParams(
            dimension_semantics=("arbitrary",), vmem_limit_bytes=VMEM_LIMIT_BYTES),
        name="finish",
    )(x2d, a, r, ga, gg, *consts)


def _relayout_w_in(w_in):
    offs = np.cumsum((0,) + SPLIT_SIZES)
    parts = [w_in[:, offs[i]:offs[i + 1]] for i in range(len(SPLIT_SIZES))]
    w_qa, w_ka, w_va, w_qi, w_ki, w_wi, w_qr, w_kr, w_vr, w_gr, w_ga, w_gg = parts
    half = KEY_DIM_R // 2
    perm = np.concatenate([
        np.arange(hh * KEY_DIM_R + part * half, hh * KEY_DIM_R + (part + 1) * half)
        for p in range(N_HEADS_R // 2) for part in (0, 1) for hh in (2 * p, 2 * p + 1)])
    w_wi_pad = jnp.pad(w_wi, ((0, 0), (HEAD_DIM_IDX, LANES - HEAD_DIM_IDX - N_HEADS_IDX)))
    w_all = jnp.concatenate(
        [w_qa, w_ka, w_va, w_qi, jnp.tile(w_ki, (1, N_HEADS_IDX)), w_wi_pad,
         w_qr[:, perm], w_kr[:, perm], w_vr, w_gr, w_ga, w_gg], axis=1)
    assert w_all.shape[1] == W_TOTAL
    return w_all.astype(BF16)


def _rotary_tables(pos):
    half = KEY_DIM_R // 2
    inv_freq = ROPE_BASE ** (-jnp.arange(half, dtype=F32) / half)
    ang = pos.astype(F32)[:, None] * inv_freq[None, :]
    cos, sin = jnp.cos(ang), jnp.sin(ang)
    return jnp.tile(cos, (1, 4)), jnp.concatenate([-sin, -sin, sin, sin], axis=1)


def _group(x, pos0, past_k, past_v, past_ki, state, w_all, g4, b4, t5_bias, log_gamma, fin_w,
           *, tm_proj, qb, sl, ret_chunk, tm_fin):
    bsz, t, _ = x.shape
    m = bsz * t
    x2d = x.reshape(m, D_MODEL)
    pos = pos0 + jnp.arange(t, dtype=jnp.int32)
    cos_t, sin_t = _rotary_tables(pos)
    n_pos_blocks = max(t // tm_proj, 1)
    if tm_proj > t:
        cos_t = jnp.tile(cos_t, (tm_proj // t, 1))
        sin_t = jnp.tile(sin_t, (tm_proj // t, 1))
    (qa, kf, kb, vf, vb, qi, kif, ki4, wi, qr, kr, vr, gr, ga, gg) = _proj(
        x2d, w_all, cos_t, sin_t, g4, b4, tm=tm_proj, n_pos_blocks=n_pos_blocks)

    def b3(a):
        return a.reshape(bsz, t, a.shape[-1])

    n_keys = t if past_k is None else past_k.shape[1] + t
    lp = -(-n_keys // sl) * sl
    kb3, vb3, ki43 = b3(kb), b3(vb), b3(ki4)
    if past_k is not None:
        pk = past_k.reshape(bsz, -1, W_A).astype(BF16)
        pv = past_v.reshape(bsz, -1, W_A).astype(BF16)
        pki = jnp.tile(past_ki, (1, 1, N_HEADS_IDX)).astype(BF16)
        kb3 = jnp.concatenate([pk, kb3], axis=1)
        vb3 = jnp.concatenate([pv, vb3], axis=1)
        ki43 = jnp.concatenate([pki, ki43], axis=1)
    if lp != n_keys:
        padw = ((0, 0), (0, lp - n_keys), (0, 0))
        kb3, vb3, ki43 = jnp.pad(kb3, padw), jnp.pad(vb3, padw), jnp.pad(ki43, padw)
    vt = vb3.reshape(bsz, lp // sl, sl, W_A).transpose(0, 1, 3, 2)
    qa3, qi3, wi3 = b3(qa), b3(qi), b3(wi)
    tq = -(-t // qb) * qb
    if tq != t:
        padq = ((0, 0), (0, tq - t), (0, 0))
        qa3, qi3, wi3 = jnp.pad(qa3, padq), jnp.pad(qi3, padq), jnp.pad(wi3, padq)
    a = _attention(qa3, qi3, wi3, kb3, vt, ki43, t5_bias, qb=qb, sl=sl, n_keys=n_keys, pos0=pos0)[:, :t]

    half = KEY_DIM_R // 2
    npair = N_HEADS_R // 2
    if state is None:
        s_init = jnp.zeros((bsz, N_HEADS_R, LANES, VAL_DIM_R), F32)
    else:
        st = state.astype(F32).reshape(bsz, npair, 2, 2, half, VAL_DIM_R)
        zero = jnp.zeros_like(st[:, :, 0])
        s_init = jnp.stack([jnp.stack([st[:, :, 0], zero], axis=3),
                            jnp.stack([zero, st[:, :, 1]], axis=3)], axis=2)
        s_init = s_init.reshape(bsz, N_HEADS_R, LANES, VAL_DIM_R)
    r, s_fin = _retention(b3(qr), b3(kr), b3(vr), b3(gr), fin_w["gn"], s_init, log_gamma, chunk=ret_chunk)
    s6 = s_fin.reshape(bsz, npair, 2, 2, 2, half, VAL_DIM_R)
    s_out = jnp.stack([s6[:, :, 0, :, 0], s6[:, :, 1, :, 1]], axis=2)
    s_out = s_out.reshape(bsz, N_HEADS_R, KEY_DIM_R, VAL_DIM_R)

    y = _finish(x2d, a.reshape(m, W_A), r.reshape(m, W_RV), ga, gg,
                fin_w["wpa"], fin_w["wpr"], fin_w["wo"], fin_w["ln1g"], fin_w["ln1b"],
                fin_w["wg"], fin_w["wu"], fin_w["wd"], fin_w["ln2g"], fin_w["ln2b"], tm=tm_fin)
    return (y.reshape(bsz, t, D_MODEL),
            kf.reshape(bsz, t, N_HEADS_A, HEAD_DIM_A), vf.reshape(bsz, t, N_HEADS_A, HEAD_DIM_A),
            kif.reshape(bsz, t, HEAD_DIM_IDX), s_out)


def kernel(x_prompt, x_sample, cache_k, cache_v, cache_idx_k, state_ret, w_in, idx_k_norm_g, idx_k_norm_b,
           t5_bias, ret_gn_g, w_pa, w_pr, w_o, ln1_g, ln1_b, w_gate, w_up, w_down, ln2_g, ln2_b):
    assert w_in.shape[0] == DEPTH
    log_gamma = jnp.log1p(-jnp.exp2(-5.0 - jnp.arange(N_HEADS_R, dtype=F32)))
    l = 0
    w_all = _relayout_w_in(w_in[l])
    g4 = jnp.tile(idx_k_norm_g[l], N_HEADS_IDX)[None, :].astype(F32)
    b4 = jnp.tile(idx_k_norm_b[l], N_HEADS_IDX)[None, :].astype(F32)
    fin_w = dict(
        gn=ret_gn_g[l][None, :].astype(F32),
        wpa=w_pa[l].astype(BF16), wpr=w_pr[l].astype(BF16), wo=w_o[l].astype(BF16),
        ln1g=ln1_g[l][None, :].astype(F32), ln1b=ln1_b[l][None, :].astype(F32),
        wg=w_gate[l].astype(BF16), wu=w_up[l].astype(BF16), wd=w_down[l].astype(BF16),
        ln2g=ln2_g[l][None, :].astype(F32), ln2b=ln2_b[l][None, :].astype(F32))

    t_p = x_prompt.shape[1]
    yp, kp, vp, ikp, sp = _group(
        x_prompt, 0, None, None, None, None, w_all, g4, b4, t5_bias, log_gamma, fin_w,
        tm_proj=min(512, t_p), qb=min(256, t_p), sl=512, ret_chunk=min(256, t_p), tm_fin=min(256, t_p))

    bs, t_s, _ = x_sample.shape
    past = cache_k.shape[2]
    ys, ks, vs, iks, ss = _group(
        x_sample, past, cache_k[l], cache_v[l], cache_idx_k[l], state_ret[l], w_all, g4, b4, t5_bias,
        log_gamma, fin_w, tm_proj=bs * t_s, qb=LANES, sl=512, ret_chunk=t_s, tm_fin=bs * t_s)

    return (yp, ys, kp[None], vp[None], ikp[None], sp[None],
            ks[None], vs[None], iks[None], ss.astype(state_ret.dtype)[None])
```

```python
import functools
import math

import numpy as np
import jax
import jax.numpy as jnp
from jax import lax
from jax.experimental import pallas as pl
from jax.experimental.pallas import tpu as pltpu

D_MODEL = 1024
CHUNK = 64
N_HEADS_A = 8
HEAD_DIM_A = 64
N_HEADS_IDX = 4
HEAD_DIM_IDX = 64
TOPK_MAX = 256
NUM_BUCKETS = 32
MAX_DISTANCE = 1024
N_HEADS_R = 8
KEY_DIM_R = 64
VAL_DIM_R = 128
ROPE_BASE = 10000.0
D_FF = 2816
DEPTH = 1
ALPHA = (2.0 * DEPTH) ** 0.25
LN_EPS = 1e-5
GN_EPS = 1e-6

W_A = N_HEADS_A * HEAD_DIM_A
W_IQ = N_HEADS_IDX * HEAD_DIM_IDX
W_RQK = N_HEADS_R * KEY_DIM_R
W_RV = N_HEADS_R * VAL_DIM_R
SPLIT_SIZES = (W_A, W_A, W_A, W_IQ, HEAD_DIM_IDX, N_HEADS_IDX, W_RQK, W_RQK, W_RV, W_RV, D_MODEL, D_MODEL)

LANES = 128
VMEM_LIMIT_BYTES = 60 * 1024 * 1024

OFF_QA = 0
OFF_KA = OFF_QA + W_A
OFF_VA = OFF_KA + W_A
OFF_QI = OFF_VA + W_A
OFF_KI4 = OFF_QI + W_IQ
OFF_WI = OFF_KI4 + W_IQ
OFF_QR = OFF_WI + LANES
OFF_KR = OFF_QR + W_RQK
OFF_VR = OFF_KR + W_RQK
OFF_GR = OFF_VR + W_RV
OFF_GA = OFF_GR + W_RV
OFF_GG = OFF_GA + D_MODEL
W_TOTAL = OFF_GG + D_MODEL

INT_MIN = -(2 ** 31)
NEG_BIG = -1e30
LOG2_E = math.log2(math.e)
FOLD_CHAINS = 8
VT_ROWS = HEAD_DIM_A + 16
BF16 = jnp.bfloat16
F32 = jnp.float32
I16 = jnp.int16
ONE16, ZERO16 = np.int16(1), np.int16(0)
MAX16, MIN16 = np.int16(2 ** 15 - 1), np.int16(-(2 ** 15))


def _dot(a, b):
    return jnp.dot(a, b, preferred_element_type=F32)


def _dot_nt(a, b):
    return lax.dot_general(a, b, (((1,), (1,)), ((), ())), preferred_element_type=F32)


def _dot_tn(a, b):
    return lax.dot_general(a, b, (((0,), (0,)), ((), ())), preferred_element_type=F32)


def _layer_norm(z, g, b):
    mu = jnp.mean(z, axis=-1, keepdims=True)
    d = z - mu
    var = jnp.mean(d * d, axis=-1, keepdims=True)
    return d * lax.rsqrt(var + LN_EPS) * g + b


def _const_spec(shape):
    nd = len(shape)
    return pl.BlockSpec(shape, lambda *_: (0,) * nd, pipeline_mode=pl.Buffered(1))


def _proj_kernel(x_ref, w_ref, cos_ref, sin_ref, g4_ref, b4_ref,
                 qa_o, kf_o, kb_o, vf_o, vb_o, qi_o, kif_o, ki4_o, wi_o,
                 qr_o, kr_o, vr_o, gr_o, ga_o, gg_o):
    xb = x_ref[...].astype(BF16)

    def seg(off, n):
        return _dot(xb, w_ref[:, off:off + n])

    qa_o[...] = (seg(OFF_QA, W_A) * (HEAD_DIM_A ** -0.5 * LOG2_E)).astype(BF16)
    k = seg(OFF_KA, W_A)
    kf_o[...] = k
    kb_o[...] = k.astype(BF16)
    v = seg(OFF_VA, W_A)
    vf_o[...] = v
    vb_o[...] = v.astype(BF16)
    qi_o[...] = seg(OFF_QI, W_IQ).astype(BF16)

    ki = seg(OFF_KI4, W_IQ)
    first = lax.broadcasted_iota(jnp.int32, ki.shape, 1) < HEAD_DIM_IDX
    inv_n = 1.0 / HEAD_DIM_IDX
    mu = jnp.sum(jnp.where(first, ki, 0.0), axis=-1, keepdims=True) * inv_n
    d = ki - mu
    var = jnp.sum(jnp.where(first, d * d, 0.0), axis=-1, keepdims=True) * inv_n
    kin = d * lax.rsqrt(var + LN_EPS) * g4_ref[...] + b4_ref[...]
    kif_o[...] = kin[:, :HEAD_DIM_IDX]
    ki4_o[...] = kin.astype(BF16)

    wi_o[...] = seg(OFF_WI, LANES) * ((N_HEADS_IDX ** -0.5) * (HEAD_DIM_IDX ** -0.5))

    for off, o_ref, scale in ((OFF_QR, qr_o, 1.0), (OFF_KR, kr_o, KEY_DIM_R ** -0.5)):
        h = seg(off, W_RQK)
        for j in range(W_RQK // LANES):
            xj = h[:, j * LANES:(j + 1) * LANES]
            rj = pltpu.roll(xj, LANES // 2, 1)
            oj = xj * cos_ref[...] + rj * sin_ref[...]
            o_ref[:, j * LANES:(j + 1) * LANES] = (oj * scale).astype(BF16)

    vr_o[...] = seg(OFF_VR, W_RV).astype(BF16)
    gr_o[...] = seg(OFF_GR, W_RV).astype(BF16)
    ga_o[...] = seg(OFF_GA, D_MODEL).astype(BF16)
    gg_o[...] = seg(OFF_GG, D_MODEL).astype(BF16)


def _proj(x2d, w_all, cos_t, sin_t, g4, b4, *, tm, n_pos_blocks):
    m = x2d.shape[0]
    grid = (m // tm,)

    def row(width, dtype):
        return jax.ShapeDtypeStruct((m, width), dtype)

    def rspec(width):
        return pl.BlockSpec((tm, width), lambda i: (i, 0))

    pos_spec = pl.BlockSpec((tm, LANES), lambda i: (i % n_pos_blocks, 0))
    out_shapes = (
        row(W_A, BF16), row(W_A, F32), row(W_A, BF16), row(W_A, F32), row(W_A, BF16),
        row(W_IQ, BF16), row(HEAD_DIM_IDX, F32), row(W_IQ, BF16), row(LANES, F32),
        row(W_RQK, BF16), row(W_RQK, BF16), row(W_RV, BF16), row(W_RV, BF16),
        row(D_MODEL, BF16), row(D_MODEL, BF16))
    out_specs = tuple(rspec(s.shape[1]) for s in out_shapes)
    return pl.pallas_call(
        _proj_kernel,
        grid=grid,
        in_specs=[rspec(D_MODEL), _const_spec((D_MODEL, W_TOTAL)), pos_spec, pos_spec,
                  _const_spec((1, W_IQ)), _const_spec((1, W_IQ))],
        out_specs=out_specs,
        out_shape=out_shapes,
        compiler_params=pltpu.CompilerParams(
            dimension_semantics=("arbitrary",), vmem_limit_bytes=VMEM_LIMIT_BYTES),
        name="proj",
    )(x2d, w_all, cos_t, sin_t, g4, b4)


def _attn_kernel(qa_ref, qi_ref, wi_ref, kb_ref, vt_ref, ki_ref, bias_ref, tri_ref, o_ref,
                 key_scr, half_scr, qt_scr, m_scr, acc_scr, lga_scr, cma_scr, lgb_scr, cmb_scr,
                 *, qb, sl, n_keys, pos0, near_off, ktop):
    i = pl.program_id(1)
    t0 = pos0 + i * qb
    lblk = jnp.minimum(n_keys, ((t0 + qb - 1) // CHUNK + 1) * CHUNK)
    nslab = (lblk + sl - 1) // sl
    n_lane_tiles = qb // LANES

    def fold8(x, op=jnp.sum):
        parts = op(x.reshape(sl // (8 * FOLD_CHAINS), FOLD_CHAINS, 8, qb), axis=0)
        return op(parts, axis=0)

    qi_t = qi_ref[...].astype(F32).T
    qi_h = [qi_t[h * HEAD_DIM_IDX:(h + 1) * HEAD_DIM_IDX].astype(BF16) for h in range(N_HEADS_IDX)]
    w_t = wi_ref[...].T
    w_h = [w_t[HEAD_DIM_IDX + h:HEAD_DIM_IDX + h + 1] for h in range(N_HEADS_IDX)]
    qpos = t0 + lax.broadcasted_iota(jnp.int32, (1, qb), 1)
    qlim = jnp.minimum((lax.shift_right_logical(qpos, int(math.log2(CHUNK))) + 1) * CHUNK, n_keys)
    key_row = lax.broadcasted_iota(jnp.int32, (sl, qb), 0)

    def a_body(j, carry):
        s0 = pl.multiple_of(j * sl, sl)
        kis = ki_ref[pl.ds(s0, sl), 0:HEAD_DIM_IDX]
        s = w_h[0] * jnp.maximum(_dot(kis, qi_h[0]), 0.0)
        for h in range(1, N_HEADS_IDX):
            s = s + w_h[h] * jnp.maximum(_dot(kis, qi_h[h]), 0.0)
        b = lax.bitcast_convert_type(s, jnp.int32)
        skey = jnp.where(b < 0, INT_MIN - b, b)
        skey = jnp.where(s0 + key_row < qlim, skey, INT_MIN)
        key_scr[j] = skey
        half_scr[j] = lax.shift_right_arithmetic(skey, 16).astype(I16)
        return carry

    lax.fori_loop(0, nslab, a_body, 0)

    def count_ge(cand_s):
        def c_body(j, acc):
            return acc + fold8(jnp.where(key_scr[j] >= cand_s, 1.0, 0.0))
        acc = lax.fori_loop(0, nslab, c_body, jnp.zeros((8, qb), F32))
        return jnp.sum(acc, axis=0, keepdims=True)

    def count_ge16(cand16):
        def c_body(j, acc):
            hit = jnp.where(half_scr[j] >= cand16, ONE16, ZERO16)
            hit = hit.reshape(sl // (16 * FOLD_CHAINS), FOLD_CHAINS, 16, qb)
            for a in range(hit.shape[0]):
                acc = acc + hit[a]
            return acc
        acc = lax.fori_loop(0, nslab, c_body, jnp.zeros((FOLD_CHAINS, 16, qb), I16))
        return jnp.sum(acc.astype(F32).reshape(FOLD_CHAINS * 16, qb), axis=0, keepdims=True)

    def radix16():
        def bit_body(it, t_u):
            cand_u = t_u | jnp.left_shift(jnp.int32(1), 15 - it)
            cnt = count_ge16((cand_u - 2 ** 15).astype(I16))
            return jnp.where(cnt >= ktop, cand_u, t_u)
        return lax.fori_loop(0, 16, bit_body, jnp.zeros((1, qb), jnp.int32))

    hi_u = radix16()
    hi16 = (hi_u - 2 ** 15).astype(I16)

    def low_body(j, carry):
        low = ((key_scr[j] & 0xFFFF) - 2 ** 15).astype(I16)
        high = half_scr[j]
        half_scr[j] = jnp.where(high == hi16, low, jnp.where(high > hi16, MAX16, MIN16))
        return carry

    lax.fori_loop(0, nslab, low_body, 0)
    thr = (hi_u - 2 ** 15) * 2 ** 16 + radix16()
    int_max = 2 ** 31 - 1
    n_gt = jnp.where(thr == int_max, 0.0, count_ge(jnp.where(thr == int_max, thr, thr + 1)))
    need = jnp.where(thr == INT_MIN, 0.0, ktop - n_gt)

    def m_body(j, carry):
        keys = key_scr[j]
        eq = keys == thr
        pref = _dot(tri_ref[...], jnp.where(eq, 1.0, 0.0).astype(BF16))
        sel = (keys > thr) | (eq & (pref + carry <= need))
        key_scr[j] = lax.bitcast_convert_type(jnp.where(sel, 0.0, NEG_BIG), jnp.int32)
        return carry + pref[sl - 1:sl]

    lax.fori_loop(0, nslab, m_body, jnp.zeros((1, qb), F32))

    jfar = jnp.minimum(jnp.maximum(t0 - near_off + sl - 1, 0) // sl, nslab)
    n_pairs = N_HEADS_A // 2
    qa_t = qa_ref[...].astype(F32).T
    low = lax.broadcasted_iota(jnp.int32, (LANES, qb), 0) < HEAD_DIM_A
    for p in range(n_pairs):
        blk = qa_t[p * LANES:(p + 1) * LANES]
        qt_scr[p, :, 0:qb] = jnp.where(low, blk, 0.0).astype(BF16)
        qt_scr[p, :, qb:2 * qb] = jnp.where(low, 0.0, blk).astype(BF16)
    m_scr[...] = jnp.full(m_scr.shape, NEG_BIG, F32)
    acc_scr[...] = jnp.zeros(acc_scr.shape, F32)

    bufs = ((lga_scr, cma_scr), (lgb_scr, cmb_scr))

    def logits_pair(j, p, buf, near):
        lg_scr, cm_scr = bufs[buf]
        s0 = pl.multiple_of(j * sl, sl)
        madd = lax.bitcast_convert_type(key_scr[j], F32)
        k2 = kb_ref[pl.ds(s0, sl), p * LANES:(p + 1) * LANES]
        l2 = _dot(k2, qt_scr[p])
        for e in range(2):
            h = 2 * p + e
            logit = l2[:, e * qb:(e + 1) * qb] + madd
            if near:
                c0 = near_off - t0 + s0 + qb - LANES + sl
                logit = logit + jnp.concatenate(
                    [bias_ref[h, pl.ds(pl.multiple_of(c0 - g * LANES, LANES), sl), :]
                     for g in range(n_lane_tiles)], axis=1)
            logit = logit.astype(BF16)
            lg_scr[h] = logit
            part = jnp.max(logit.reshape(sl // (16 * FOLD_CHAINS), FOLD_CHAINS, 16, qb), axis=0)
            cm_scr[h] = jnp.max(jnp.max(part, axis=0).astype(F32), axis=0, keepdims=True)

    def softmax_head(j, h, buf):
        lg_scr, cm_scr = bufs[buf]
        m_prev = m_scr[h]
        m_new = jnp.maximum(m_prev, cm_scr[h])
        alpha = jnp.exp2(m_prev - m_new)
        pexp = jnp.exp2(lg_scr[h] - m_new.astype(BF16))
        acc_scr[h] = alpha * acc_scr[h] + _dot(vt_ref[j, h], pexp)
        m_scr[h] = m_new

    def logits_stage(j, buf, near):
        for p in range(n_pairs):
            logits_pair(j, p, buf, near)

    def softmax_stage(j, buf):
        for h in range(N_HEADS_A):
            softmax_head(j, h, buf)

    def overlapped(j_soft, buf_soft, j_logits, near):
        for p in range(n_pairs):
            logits_pair(j_logits, p, 1 - buf_soft, near)
            softmax_head(j_soft, 2 * p, buf_soft)
            softmax_head(j_soft, 2 * p + 1, buf_soft)

    def d_body(jj, carry, near):
        j = 2 * jj
        overlapped(j, 0, j + 1, near)
        overlapped(j + 1, 1, j + 2, near)
        return carry

    def odd_tail():
        softmax_stage(nslab - 1, 0)

    def even_tail():
        overlapped(nslab - 2, 0, nslab - 1, True)
        softmax_stage(nslab - 1, 1)

    lax.cond(jfar > 0, lambda: logits_stage(0, 0, False), lambda: logits_stage(0, 0, True))
    n_trips = (nslab - 1) // 2
    far_trips = jnp.minimum(jnp.maximum(jfar - 1, 0) // 2, n_trips)
    lax.fori_loop(0, far_trips, functools.partial(d_body, near=False), 0)
    lax.fori_loop(far_trips, n_trips, functools.partial(d_body, near=True), 0)
    lax.cond(lax.rem(nslab, 2) == 1, odd_tail, even_tail)

    for p in range(n_pairs):
        outs = [acc_scr[h, 0:HEAD_DIM_A] / acc_scr[h, HEAD_DIM_A:HEAD_DIM_A + 1] for h in (2 * p, 2 * p + 1)]
        o_ref[:, p * LANES:(p + 1) * LANES] = jnp.concatenate(outs, axis=0).T.astype(o_ref.dtype)


def _t5_bucket_np(rel):
    nb = NUM_BUCKETS // 2
    max_exact = nb // 2
    ret = np.where(rel > 0, nb, 0)
    n = np.abs(rel)
    nf = np.maximum(n, max_exact).astype(np.float64)
    large = max_exact + (np.log(nf / max_exact) / math.log(MAX_DISTANCE / max_exact) * (nb - max_exact)).astype(np.int64)
    large = np.minimum(large, nb - 1)
    return ret + np.where(n < max_exact, n, large)


def _t5_bucket(rel):
    nb = NUM_BUCKETS // 2
    max_exact = nb // 2
    ret = jnp.where(rel > 0, nb, 0)
    n = jnp.abs(rel)
    nf = jnp.maximum(n, max_exact).astype(F32)
    large = max_exact + (jnp.log(nf / max_exact) / math.log(MAX_DISTANCE / max_exact) * (nb - max_exact)).astype(jnp.int32)
    large = jnp.minimum(large, nb - 1)
    return ret + jnp.where(n < max_exact, n, large)


def _attention(qa, qi, wi, kb, vt, ki4, t5_bias, *, qb, sl, n_keys, pos0):
    bsz, tq, _ = qa.shape
    lp = kb.shape[1]
    assert tq % qb == 0 and lp % sl == 0 and sl % LANES == 0 and qb % LANES == 0
    ktop = min(TOPK_MAX, n_keys // 4)
    nslab_max = lp // sl

    gran = math.gcd(pos0, sl) if tq == qb else math.gcd(math.gcd(pos0, qb), sl)
    assert gran % LANES == 0
    rel_all = np.arange(-(n_keys + qb), 0)
    far_bucket = _t5_bucket_np(np.array([-(n_keys + qb)]))[0]
    sat = rel_all[_t5_bucket_np(rel_all) != far_bucket]
    n_sat = int(-sat.min()) + 1 if sat.size else 1
    near_off = -(-(sl - 1 - gran + n_sat) // gran) * gran
    rel_min = -(qb - 1) - near_off
    rels = jnp.arange(rel_min, sl, dtype=jnp.int32)
    tab = (t5_bias[_t5_bucket(rels)] - t5_bias[far_bucket][None, :]) * LOG2_E
    n_rel = sl - rel_min
    n_rows = near_off + sl + qb - LANES
    assert n_rows + LANES - 1 == n_rel
    hank = jnp.tile(tab.T, (1, LANES + 1))[:, :LANES * (n_rel + 1)].reshape(N_HEADS_A, LANES, n_rel + 1)
    master = jnp.transpose(hank[:, ::-1, :n_rows], (0, 2, 1))
    master = jnp.pad(master, ((0, 0), (sl, 0), (0, 0)))

    tri = jnp.asarray(np.tril(np.ones((sl, sl), np.float32)), dtype=BF16)

    kern = functools.partial(_attn_kernel, qb=qb, sl=sl, n_keys=n_keys, pos0=pos0,
                             near_off=near_off, ktop=float(ktop))

    def qspec(width_):
        return pl.BlockSpec((None, qb, width_), lambda b, i: (b, i, 0))

    def kspec(*shape):
        nd = len(shape)
        return pl.BlockSpec((None,) + shape, lambda b, i: (b,) + (0,) * nd, pipeline_mode=pl.Buffered(1))

    return pl.pallas_call(
        kern,
        grid=(bsz, tq // qb),
        in_specs=[qspec(W_A), qspec(W_IQ), qspec(LANES), kspec(lp, W_A),
                  kspec(nslab_max, N_HEADS_A, VT_ROWS, sl),
                  kspec(lp, W_IQ), _const_spec(master.shape), _const_spec((sl, sl))],
        out_specs=qspec(W_A),
        out_shape=jax.ShapeDtypeStruct((bsz, tq, W_A), BF16),
        scratch_shapes=[pltpu.VMEM((nslab_max, sl, qb), jnp.int32),
                        pltpu.VMEM((nslab_max, sl, qb), I16),
                        pltpu.VMEM((N_HEADS_A // 2, LANES, 2 * qb), BF16),
                        pltpu.VMEM((N_HEADS_A, 1, qb), F32),
                        pltpu.VMEM((N_HEADS_A, VT_ROWS, qb), F32),
                        pltpu.VMEM((N_HEADS_A, sl, qb), BF16), pltpu.VMEM((N_HEADS_A, 1, qb), F32),
                        pltpu.VMEM((N_HEADS_A, sl, qb), BF16), pltpu.VMEM((N_HEADS_A, 1, qb), F32)],
        compiler_params=pltpu.CompilerParams(
            dimension_semantics=("arbitrary", "arbitrary"), vmem_limit_bytes=VMEM_LIMIT_BYTES),
        name="dsa_attention",
    )(qa, qi, wi, kb, vt, ki4, master, tri)


def _ret_kernel(q_ref, k_ref, v_ref, g_ref, gn_ref, s0_ref, dm_ref, qd_ref, kd_ref, gc_ref,
                r_ref, sfin_ref, s_scr, *, n_chunks):
    c = pl.program_id(1)

    @pl.when(c == 0)
    def _():
        s_scr[...] = s0_ref[...]

    for h in range(N_HEADS_R):
        p = h // 2
        q2 = q_ref[:, p * LANES:(p + 1) * LANES].astype(F32)
        k2 = k_ref[:, p * LANES:(p + 1) * LANES]
        vh = v_ref[:, h * VAL_DIM_R:(h + 1) * VAL_DIM_R]
        s_h = s_scr[h]
        qm = (q2 * qd_ref[h, 0]).astype(BF16)
        qdec = (q2 * qd_ref[h, 1]).astype(BF16)
        kdec = (k2.astype(F32) * kd_ref[h]).astype(BF16)
        inner = _dot_nt(qm, k2) * dm_ref[h]
        o = _dot(inner.astype(BF16), vh) + _dot(qdec, s_h.astype(BF16))
        s_scr[h] = gc_ref[h] * s_h + _dot_tn(kdec, vh)
        mu = jnp.mean(o, axis=-1, keepdims=True)
        d = o - mu
        var = jnp.mean(d * d, axis=-1, keepdims=True)
        on = d * lax.rsqrt(var + GN_EPS) * gn_ref[:, h * VAL_DIM_R:(h + 1) * VAL_DIM_R]
        gate = g_ref[:, h * VAL_DIM_R:(h + 1) * VAL_DIM_R].astype(F32)
        r_ref[:, h * VAL_DIM_R:(h + 1) * VAL_DIM_R] = (on * (gate * jax.nn.sigmoid(gate))).astype(r_ref.dtype)

    @pl.when(c == n_chunks - 1)
    def _():
        sfin_ref[...] = s_scr[...]


def _pair_lane_heads():
    return (np.arange(LANES) // (KEY_DIM_R // 2)) % 2


def _retention(qr, kr, vr, gr, gn_g, s_init, log_gamma, *, chunk):
    bsz, t, _ = qr.shape
    n_chunks = t // chunk
    n = jnp.arange(chunk, dtype=F32)
    lg = log_gamma.astype(F32)
    diff = n[:, None] - n[None, :]
    dmask = jnp.where(diff >= 0, jnp.exp(lg[:, None, None] * jnp.maximum(diff, 0.0)), 0.0)
    owner = jnp.asarray(_pair_lane_heads()[None, :] == (np.arange(N_HEADS_R) % 2)[:, None], F32)
    q_dec = jnp.exp(lg[:, None] * (n + 1.0))
    k_dec = jnp.exp(lg[:, None] * (chunk - 1.0 - n))
    qd = jnp.stack([jnp.broadcast_to(owner[:, None, :], (N_HEADS_R, chunk, LANES)),
                    owner[:, None, :] * q_dec[:, :, None]], axis=1)
    kd = owner[:, None, :] * k_dec[:, :, None]
    gc = jnp.broadcast_to(jnp.exp(lg * chunk)[:, None, None], (N_HEADS_R, 1, LANES))

    def tspec(width):
        return pl.BlockSpec((None, chunk, width), lambda b, c: (b, c, 0))

    sspec = pl.BlockSpec((None, N_HEADS_R, LANES, VAL_DIM_R), lambda b, c: (b, 0, 0, 0))
    return pl.pallas_call(
        functools.partial(_ret_kernel, n_chunks=n_chunks),
        grid=(bsz, n_chunks),
        in_specs=[tspec(W_RQK), tspec(W_RQK), tspec(W_RV), tspec(W_RV), _const_spec((1, W_RV)), sspec,
                  _const_spec(dmask.shape), _const_spec(qd.shape), _const_spec(kd.shape),
                  _const_spec(gc.shape)],
        out_specs=(tspec(W_RV), sspec),
        out_shape=(jax.ShapeDtypeStruct((bsz, t, W_RV), BF16),
                   jax.ShapeDtypeStruct((bsz, N_HEADS_R, LANES, VAL_DIM_R), F32)),
        scratch_shapes=[pltpu.VMEM((N_HEADS_R, LANES, VAL_DIM_R), F32)],
        compiler_params=pltpu.CompilerParams(
            dimension_semantics=("arbitrary", "arbitrary"), vmem_limit_bytes=VMEM_LIMIT_BYTES),
        name="retention",
    )(qr, kr, vr, gr, gn_g, s_init, dmask, qd, kd, gc)


def _finish_kernel(x_ref, a_ref, r_ref, ga_ref, gg_ref, wpa, wpr, wo, ln1g, ln1b, wg, wu, wd, ln2g, ln2b,
                   y_ref):
    x = x_ref[...]
    merged = (jax.nn.sigmoid(ga_ref[...].astype(F32)) * _dot(a_ref[...], wpa[...])
              + jax.nn.sigmoid(gg_ref[...].astype(F32)) * _dot(r_ref[...], wpr[...]))
    x1 = _layer_norm(ALPHA * x + _dot(merged.astype(BF16), wo[...]), ln1g[...], ln1b[...])
    x1b = x1.astype(BF16)
    gate = _dot(x1b, wg[...])
    hidden = gate * jax.nn.sigmoid(gate) * _dot(x1b, wu[...])
    y = _layer_norm(ALPHA * x1 + _dot(hidden.astype(BF16), wd[...]), ln2g[...], ln2b[...])
    y_ref[...] = y


def _finish(x2d, a, r, ga, gg, wpa, wpr, wo, ln1g, ln1b, wg, wu, wd, ln2g, ln2b, *, tm):
    m = x2d.shape[0]

    def rspec(width):
        return pl.BlockSpec((tm, width), lambda i: (i, 0))

    consts = (wpa, wpr, wo, ln1g, ln1b, wg, wu, wd, ln2g, ln2b)
    return pl.pallas_call(
        _finish_kernel,
        grid=(m // tm,),
        in_specs=[rspec(D_MODEL), rspec(W_A), rspec(W_RV), rspec(D_MODEL), rspec(D_MODEL)]
        + [_const_spec(c.shape) for c in consts],
        out_specs=rspec(D_MODEL),
        out_shape=jax.ShapeDtypeStruct((m, D_MODEL), F32),
        compiler_params=pltpu.CompilerParams(
            dimension_semantics=("arbitrary",), vmem_limit_bytes=VMEM_LIMIT_BYTES),
        name="finish",
    )(x2d, a, r, ga, gg, *consts)


def _relayout_w_in(w_in):
    offs = np.cumsum((0,) + SPLIT_SIZES)
    parts = [w_in[:, offs[i]:offs[i + 1]] for i in range(len(SPLIT_SIZES))]
    w_qa, w_ka, w_va, w_qi, w_ki, w_wi, w_qr, w_kr, w_vr, w_gr, w_ga, w_gg = parts
    half = KEY_DIM_R // 2
    perm = np.concatenate([
        np.arange(hh * KEY_DIM_R + part * half, hh * KEY_DIM_R + (part + 1) * half)
        for p in range(N_HEADS_R // 2) for part in (0, 1) for hh in (2 * p, 2 * p + 1)])
    w_wi_pad = jnp.pad(w_wi, ((0, 0), (HEAD_DIM_IDX, LANES - HEAD_DIM_IDX - N_HEADS_IDX)))
    w_all = jnp.concatenate(
        [w_qa, w_ka, w_va, w_qi, jnp.tile(w_ki, (1, N_HEADS_IDX)), w_wi_pad,
         w_qr[:, perm], w_kr[:, perm], w_vr, w_gr, w_ga, w_gg], axis=1)
    assert w_all.shape[1] == W_TOTAL
    return w_all.astype(BF16)


def _rotary_tables(pos):
    half = KEY_DIM_R // 2
    inv_freq = ROPE_BASE ** (-jnp.arange(half, dtype=F32) / half)
    ang = pos.astype(F32)[:, None] * inv_freq[None, :]
    cos, sin = jnp.cos(ang), jnp.sin(ang)
    return jnp.tile(cos, (1, 4)), jnp.concatenate([-sin, -sin, sin, sin], axis=1)


def _group(x, pos0, past_k, past_v, past_ki, state, w_all, g4, b4, t5_bias, log_gamma, fin_w,
           *, tm_proj, qb, sl, ret_chunk, tm_fin):
    bsz, t, _ = x.shape
    m = bsz * t
    x2d = x.reshape(m, D_MODEL)
    pos = pos0 + jnp.arange(t, dtype=jnp.int32)
    cos_t, sin_t = _rotary_tables(pos)
    n_pos_blocks = max(t // tm_proj, 1)
    if tm_proj > t:
        cos_t = jnp.tile(cos_t, (tm_proj // t, 1))
        sin_t = jnp.tile(sin_t, (tm_proj // t, 1))
    (qa, kf, kb, vf, vb, qi, kif, ki4, wi, qr, kr, vr, gr, ga, gg) = _proj(
        x2d, w_all, cos_t, sin_t, g4, b4, tm=tm_proj, n_pos_blocks=n_pos_blocks)

    def b3(a):
        return a.reshape(bsz, t, a.shape[-1])

    n_keys = t if past_k is None else past_k.shape[1] + t
    lp = -(-n_keys // sl) * sl
    kb3, vb3, ki43 = b3(kb), b3(vb), b3(ki4)
    if past_k is not None:
        pk = past_k.reshape(bsz, -1, W_A).astype(BF16)
        pv = past_v.reshape(bsz, -1, W_A).astype(BF16)
        pki = jnp.tile(past_ki, (1, 1, N_HEADS_IDX)).astype(BF16)
        kb3 = jnp.concatenate([pk, kb3], axis=1)
        vb3 = jnp.concatenate([pv, vb3], axis=1)
        ki43 = jnp.concatenate([pki, ki43], axis=1)
    if lp != n_keys:
        padw = ((0, 0), (0, lp - n_keys), (0, 0))
        kb3, vb3, ki43 = jnp.pad(kb3, padw), jnp.pad(vb3, padw), jnp.pad(ki43, padw)
    vt = vb3.reshape(bsz, lp // sl, sl, N_HEADS_A, HEAD_DIM_A).transpose(0, 1, 3, 4, 2)
    ones = jnp.ones((bsz, lp // sl, N_HEADS_A, VT_ROWS - HEAD_DIM_A, sl), BF16)
    vt = jnp.concatenate([vt, ones], axis=3)
    qa3, qi3, wi3 = b3(qa), b3(qi), b3(wi)
    tq = -(-t // qb) * qb
    if tq != t:
        padq = ((0, 0), (0, tq - t), (0, 0))
        qa3, qi3, wi3 = jnp.pad(qa3, padq), jnp.pad(qi3, padq), jnp.pad(wi3, padq)
    a = _attention(qa3, qi3, wi3, kb3, vt, ki43, t5_bias, qb=qb, sl=sl, n_keys=n_keys, pos0=pos0)[:, :t]

    half = KEY_DIM_R // 2
    npair = N_HEADS_R // 2
    if state is None:
        s_init = jnp.zeros((bsz, N_HEADS_R, LANES, VAL_DIM_R), F32)
    else:
        st = state.astype(F32).reshape(bsz, npair, 2, 2, half, VAL_DIM_R)
        zero = jnp.zeros_like(st[:, :, 0])
        s_init = jnp.stack([jnp.stack([st[:, :, 0], zero], axis=3),
                            jnp.stack([zero, st[:, :, 1]], axis=3)], axis=2)
        s_init = s_init.reshape(bsz, N_HEADS_R, LANES, VAL_DIM_R)
    r, s_fin = _retention(b3(qr), b3(kr), b3(vr), b3(gr), fin_w["gn"], s_init, log_gamma, chunk=ret_chunk)
    s6 = s_fin.reshape(bsz, npair, 2, 2, 2, half, VAL_DIM_R)
    s_out = jnp.stack([s6[:, :, 0, :, 0], s6[:, :, 1, :, 1]], axis=2)
    s_out = s_out.reshape(bsz, N_HEADS_R, KEY_DIM_R, VAL_DIM_R)

    y = _finish(x2d, a.reshape(m, W_A), r.reshape(m, W_RV), ga, gg,
                fin_w["wpa"], fin_w["wpr"], fin_w["wo"], fin_w["ln1g"], fin_w["ln1b"],
                fin_w["wg"], fin_w["wu"], fin_w["wd"], fin_w["ln2g"], fin_w["ln2b"], tm=tm_fin)
    return (y.reshape(bsz, t, D_MODEL),
            kf.reshape(bsz, t, N_HEADS_A, HEAD_DIM_A), vf.reshape(bsz, t, N_HEADS_A, HEAD_DIM_A),
            kif.reshape(bsz, t, HEAD_DIM_IDX), s_out)


def kernel(x_prompt, x_sample, cache_k, cache_v, cache_idx_k, state_ret, w_in, idx_k_norm_g, idx_k_norm_b,
           t5_bias, ret_gn_g, w_pa, w_pr, w_o, ln1_g, ln1_b, w_gate, w_up, w_down, ln2_g, ln2_b):
    assert w_in.shape[0] == DEPTH
    log_gamma = jnp.log1p(-jnp.exp2(-5.0 - jnp.arange(N_HEADS_R, dtype=F32)))
    l = 0
    w_all = _relayout_w_in(w_in[l])
    g4 = jnp.tile(idx_k_norm_g[l], N_HEADS_IDX)[None, :].astype(F32)
    b4 = jnp.tile(idx_k_norm_b[l], N_HEADS_IDX)[None, :].astype(F32)
    fin_w = dict(
        gn=ret_gn_g[l][None, :].astype(F32),
        wpa=w_pa[l].astype(BF16), wpr=w_pr[l].astype(BF16), wo=w_o[l].astype(BF16),
        ln1g=ln1_g[l][None, :].astype(F32), ln1b=ln1_b[l][None, :].astype(F32),
        wg=w_gate[l].astype(BF16), wu=w_up[l].astype(BF16), wd=w_down[l].astype(BF16),
        ln2g=ln2_g[l][None, :].astype(F32), ln2b=ln2_b[l][None, :].astype(F32))

    t_p = x_prompt.shape[1]
    yp, kp, vp, ikp, sp = _group(
        x_prompt, 0, None, None, None, None, w_all, g4, b4, t5_bias, log_gamma, fin_w,
        tm_proj=min(512, t_p), qb=min(256, t_p), sl=512, ret_chunk=min(256, t_p), tm_fin=min(256, t_p))

    bs, t_s, _ = x_sample.shape
    past = cache_k.shape[2]
    ys, ks, vs, iks, ss = _group(
        x_sample, past, cache_k[l], cache_v[l], cache_idx_k[l], state_ret[l], w_all, g4, b4, t5_bias,
        log_gamma, fin_w, tm_proj=bs * t_s, qb=LANES, sl=512, ret_chunk=t_s, tm_fin=bs * t_s)

    return (yp, ys, kp[None], vp[None], ikp[None], sp[None],
            ks[None], vs[None], iks[None], ss.astype(state_ret.dtype)[None])
```

```python
import functools
import math

import numpy as np
import jax
import jax.numpy as jnp
from jax import lax
from jax.experimental import pallas as pl
from jax.experimental.pallas import tpu as pltpu

D_MODEL = 1024
CHUNK = 64
N_HEADS_A = 8
HEAD_DIM_A = 64
N_HEADS_IDX = 4
HEAD_DIM_IDX = 64
TOPK_MAX = 256
NUM_BUCKETS = 32
MAX_DISTANCE = 1024
N_HEADS_R = 8
KEY_DIM_R = 64
VAL_DIM_R = 128
ROPE_BASE = 10000.0
D_FF = 2816
DEPTH = 1
ALPHA = (2.0 * DEPTH) ** 0.25
LN_EPS = 1e-5
GN_EPS = 1e-6

W_A = N_HEADS_A * HEAD_DIM_A
W_IQ = N_HEADS_IDX * HEAD_DIM_IDX
W_RQK = N_HEADS_R * KEY_DIM_R
W_RV = N_HEADS_R * VAL_DIM_R
SPLIT_SIZES = (W_A, W_A, W_A, W_IQ, HEAD_DIM_IDX, N_HEADS_IDX, W_RQK, W_RQK, W_RV, W_RV, D_MODEL, D_MODEL)

LANES = 128
VMEM_LIMIT_BYTES = 60 * 1024 * 1024

OFF_QA = 0
OFF_KA = OFF_QA + W_A
OFF_VA = OFF_KA + W_A
OFF_QI = OFF_VA + W_A
OFF_KI4 = OFF_QI + W_IQ
OFF_WI = OFF_KI4 + W_IQ
OFF_QR = OFF_WI + LANES
OFF_KR = OFF_QR + W_RQK
OFF_VR = OFF_KR + W_RQK
OFF_GR = OFF_VR + W_RV
OFF_GA = OFF_GR + W_RV
OFF_GG = OFF_GA + D_MODEL
W_TOTAL = OFF_GG + D_MODEL

INT_MIN = -(2 ** 31)
NEG_BIG = -1e30
LOG2_E = math.log2(math.e)
FOLD_CHAINS = 8
VT_ROWS = HEAD_DIM_A + 16
BF16 = jnp.bfloat16
F32 = jnp.float32
I16 = jnp.int16
ONE16, ZERO16 = np.int16(1), np.int16(0)
MAX16, MIN16 = np.int16(2 ** 15 - 1), np.int16(-(2 ** 15))


def _dot(a, b):
    return jnp.dot(a, b, preferred_element_type=F32)


def _dot_nt(a, b):
    return lax.dot_general(a, b, (((1,), (1,)), ((), ())), preferred_element_type=F32)


def _dot_tn(a, b):
    return lax.dot_general(a, b, (((0,), (0,)), ((), ())), preferred_element_type=F32)


def _layer_norm(z, g, b):
    mu = jnp.mean(z, axis=-1, keepdims=True)
    d = z - mu
    var = jnp.mean(d * d, axis=-1, keepdims=True)
    return d * lax.rsqrt(var + LN_EPS) * g + b


def _const_spec(shape):
    nd = len(shape)
    return pl.BlockSpec(shape, lambda *_: (0,) * nd, pipeline_mode=pl.Buffered(1))


def _proj_kernel(x_ref, w_ref, cos_ref, sin_ref, g4_ref, b4_ref,
                 qa_o, kf_o, kb_o, vf_o, vb_o, qi_o, kif_o, ki4_o, wi_o,
                 qr_o, kr_o, vr_o, gr_o, ga_o, gg_o, *, emit_vt):
    xb = x_ref[...].astype(BF16)

    def seg(off, n):
        return _dot(xb, w_ref[:, off:off + n])

    qa_o[...] = (seg(OFF_QA, W_A) * (HEAD_DIM_A ** -0.5 * LOG2_E)).astype(BF16)
    k = seg(OFF_KA, W_A)
    kf_o[...] = k.reshape(kf_o.shape)
    kb_o[...] = k.astype(BF16)
    v = seg(OFF_VA, W_A)
    vf_o[...] = v.reshape(vf_o.shape)
    if emit_vt:
        v_t = v.T
        for h in range(N_HEADS_A):
            vb_o[h, 0:HEAD_DIM_A, :] = v_t[h * HEAD_DIM_A:(h + 1) * HEAD_DIM_A].astype(BF16)
            vb_o[h, HEAD_DIM_A:VT_ROWS, :] = jnp.ones((VT_ROWS - HEAD_DIM_A, v_t.shape[1]), BF16)
    else:
        vb_o[...] = v.astype(BF16)
    qi_o[...] = seg(OFF_QI, W_IQ).astype(BF16)

    ki = seg(OFF_KI4, W_IQ)
    first = lax.broadcasted_iota(jnp.int32, ki.shape, 1) < HEAD_DIM_IDX
    inv_n = 1.0 / HEAD_DIM_IDX
    mu = jnp.sum(jnp.where(first, ki, 0.0), axis=-1, keepdims=True) * inv_n
    d = ki - mu
    var = jnp.sum(jnp.where(first, d * d, 0.0), axis=-1, keepdims=True) * inv_n
    kin = d * lax.rsqrt(var + LN_EPS) * g4_ref[...] + b4_ref[...]
    kif_o[...] = kin[:, :HEAD_DIM_IDX]
    ki4_o[...] = kin.astype(BF16)

    wi_o[...] = seg(OFF_WI, LANES) * ((N_HEADS_IDX ** -0.5) * (HEAD_DIM_IDX ** -0.5))

    for off, o_ref, scale in ((OFF_QR, qr_o, 1.0), (OFF_KR, kr_o, KEY_DIM_R ** -0.5)):
        h = seg(off, W_RQK)
        for j in range(W_RQK // LANES):
            xj = h[:, j * LANES:(j + 1) * LANES]
            rj = pltpu.roll(xj, LANES // 2, 1)
            oj = xj * cos_ref[...] + rj * sin_ref[...]
            o_ref[:, j * LANES:(j + 1) * LANES] = (oj * scale).astype(BF16)

    vr_o[...] = seg(OFF_VR, W_RV).astype(BF16)
    gr_o[...] = seg(OFF_GR, W_RV).astype(BF16)
    ga_o[...] = seg(OFF_GA, D_MODEL).astype(BF16)
    gg_o[...] = seg(OFF_GG, D_MODEL).astype(BF16)


def _proj(x2d, w_all, cos_t, sin_t, g4, b4, *, tm, n_pos_blocks, emit_vt):
    m = x2d.shape[0]
    grid = (m // tm,)

    def row(width, dtype):
        return jax.ShapeDtypeStruct((m, width), dtype)

    def rspec(width):
        return pl.BlockSpec((tm, width), lambda i: (i, 0))

    pos_spec = pl.BlockSpec((tm, LANES), lambda i: (i % n_pos_blocks, 0))
    out_shapes = [
        row(W_A, BF16), row(W_A, F32), row(W_A, BF16), row(W_A, F32), row(W_A, BF16),
        row(W_IQ, BF16), row(HEAD_DIM_IDX, F32), row(W_IQ, BF16), row(LANES, F32),
        row(W_RQK, BF16), row(W_RQK, BF16), row(W_RV, BF16), row(W_RV, BF16),
        row(D_MODEL, BF16), row(D_MODEL, BF16)]
    out_specs = [rspec(s.shape[1]) for s in out_shapes]
    for idx in (1, 3):
        out_shapes[idx] = jax.ShapeDtypeStruct((m, N_HEADS_A, HEAD_DIM_A), F32)
        out_specs[idx] = pl.BlockSpec((tm, N_HEADS_A, HEAD_DIM_A), lambda i: (i, 0, 0))
    if emit_vt:
        out_shapes[4] = jax.ShapeDtypeStruct((m // tm, N_HEADS_A, VT_ROWS, tm), BF16)
        out_specs[4] = pl.BlockSpec((None, N_HEADS_A, VT_ROWS, tm), lambda i: (i, 0, 0, 0))
    return pl.pallas_call(
        functools.partial(_proj_kernel, emit_vt=emit_vt),
        grid=grid,
        in_specs=[rspec(D_MODEL), _const_spec((D_MODEL, W_TOTAL)), pos_spec, pos_spec,
                  _const_spec((1, W_IQ)), _const_spec((1, W_IQ))],
        out_specs=tuple(out_specs),
        out_shape=tuple(out_shapes),
        compiler_params=pltpu.CompilerParams(
            dimension_semantics=("arbitrary",), vmem_limit_bytes=VMEM_LIMIT_BYTES),
        name="proj",
    )(x2d, w_all, cos_t, sin_t, g4, b4)


def _attn_kernel(qa_ref, qi_ref, wi_ref, kb_ref, vt_ref, ki_ref, bias_ref, tri_ref, o_ref,
                 key_scr, half_scr, qt_scr, m_scr, acc_scr, lga_scr, cma_scr, lgb_scr, cmb_scr,
                 *, qb, sl, n_keys, pos0, near_off, ktop):
    i = pl.program_id(1)
    t0 = pos0 + i * qb
    lblk = jnp.minimum(n_keys, ((t0 + qb - 1) // CHUNK + 1) * CHUNK)
    nslab = (lblk + sl - 1) // sl
    n_lane_tiles = qb // LANES

    def fold8(x, op=jnp.sum):
        parts = op(x.reshape(sl // (8 * FOLD_CHAINS), FOLD_CHAINS, 8, qb), axis=0)
        return op(parts, axis=0)

    qi_t = qi_ref[...].astype(F32).T
    qi_h = [qi_t[h * HEAD_DIM_IDX:(h + 1) * HEAD_DIM_IDX].astype(BF16) for h in range(N_HEADS_IDX)]
    w_t = wi_ref[...].T
    w_h = [w_t[HEAD_DIM_IDX + h:HEAD_DIM_IDX + h + 1] for h in range(N_HEADS_IDX)]
    qpos = t0 + lax.broadcasted_iota(jnp.int32, (1, qb), 1)
    qlim = jnp.minimum((lax.shift_right_logical(qpos, int(math.log2(CHUNK))) + 1) * CHUNK, n_keys)
    key_row = lax.broadcasted_iota(jnp.int32, (sl, qb), 0)

    def a_body(j, carry):
        s0 = pl.multiple_of(j * sl, sl)
        kis = ki_ref[pl.ds(s0, sl), 0:HEAD_DIM_IDX]
        s = w_h[0] * jnp.maximum(_dot(kis, qi_h[0]), 0.0)
        for h in range(1, N_HEADS_IDX):
            s = s + w_h[h] * jnp.maximum(_dot(kis, qi_h[h]), 0.0)
        b = lax.bitcast_convert_type(s, jnp.int32)
        skey = jnp.where(b < 0, INT_MIN - b, b)
        skey = jnp.where(s0 + key_row < qlim, skey, INT_MIN)
        key_scr[j] = skey
        half_scr[j] = lax.shift_right_arithmetic(skey, 16).astype(I16)
        return carry

    lax.fori_loop(0, nslab, a_body, 0)
    half_scr[nslab] = jnp.full((sl, qb), MIN16, I16)

    def count_ge(cand_s):
        def c_body(j, acc):
            return acc + fold8(jnp.where(key_scr[j] >= cand_s, 1.0, 0.0))
        acc = lax.fori_loop(0, nslab, c_body, jnp.zeros((8, qb), F32))
        return jnp.sum(acc, axis=0, keepdims=True)

    def count_ge16(cand16):
        def c_body(jj, acc):
            for j in (2 * jj, 2 * jj + 1):
                hit = jnp.where(half_scr[j] >= cand16, ONE16, ZERO16)
                hit = hit.reshape(sl // (16 * FOLD_CHAINS), FOLD_CHAINS, 16, qb)
                for a in range(hit.shape[0]):
                    acc = acc + hit[a]
            return acc
        acc = lax.fori_loop(0, (nslab + 1) // 2, c_body, jnp.zeros((FOLD_CHAINS, 16, qb), I16))
        return jnp.sum(acc.astype(F32).reshape(FOLD_CHAINS * 16, qb), axis=0, keepdims=True)

    def radix16():
        def bit_body(it, t_u):
            cand_u = t_u | jnp.left_shift(jnp.int32(1), 15 - it)
            cnt = count_ge16((cand_u - 2 ** 15).astype(I16))
            return jnp.where(cnt >= ktop, cand_u, t_u)
        return lax.fori_loop(0, 16, bit_body, jnp.zeros((1, qb), jnp.int32))

    hi_u = radix16()
    hi16 = (hi_u - 2 ** 15).astype(I16)

    def low_body(j, carry):
        low = ((key_scr[j] & 0xFFFF) - 2 ** 15).astype(I16)
        high = half_scr[j]
        half_scr[j] = jnp.where(high == hi16, low, jnp.where(high > hi16, MAX16, MIN16))
        return carry

    lax.fori_loop(0, nslab, low_body, 0)
    thr = (hi_u - 2 ** 15) * 2 ** 16 + radix16()
    int_max = 2 ** 31 - 1
    n_gt = jnp.where(thr == int_max, 0.0, count_ge(jnp.where(thr == int_max, thr, thr + 1)))
    need = jnp.where(thr == INT_MIN, 0.0, ktop - n_gt)

    def m_body(j, carry):
        keys = key_scr[j]
        eq = keys == thr
        pref = _dot(tri_ref[...], jnp.where(eq, 1.0, 0.0).astype(BF16))
        sel = (keys > thr) | (eq & (pref + carry <= need))
        key_scr[j] = lax.bitcast_convert_type(jnp.where(sel, 0.0, NEG_BIG), jnp.int32)
        return carry + pref[sl - 1:sl]

    lax.fori_loop(0, nslab, m_body, jnp.zeros((1, qb), F32))

    jfar = jnp.minimum(jnp.maximum(t0 - near_off + sl - 1, 0) // sl, nslab)
    n_pairs = N_HEADS_A // 2
    qa_t = qa_ref[...].astype(F32).T
    low = lax.broadcasted_iota(jnp.int32, (LANES, qb), 0) < HEAD_DIM_A
    for p in range(n_pairs):
        blk = qa_t[p * LANES:(p + 1) * LANES]
        qt_scr[p, :, 0:qb] = jnp.where(low, blk, 0.0).astype(BF16)
        qt_scr[p, :, qb:2 * qb] = jnp.where(low, 0.0, blk).astype(BF16)
    m_scr[...] = jnp.full(m_scr.shape, NEG_BIG, F32)
    acc_scr[...] = jnp.zeros(acc_scr.shape, F32)

    bufs = ((lga_scr, cma_scr), (lgb_scr, cmb_scr))

    def logits_pair(j, p, buf, near):
        lg_scr, cm_scr = bufs[buf]
        s0 = pl.multiple_of(j * sl, sl)
        madd = lax.bitcast_convert_type(key_scr[j], F32)
        k2 = kb_ref[pl.ds(s0, sl), p * LANES:(p + 1) * LANES]
        l2 = _dot(k2, qt_scr[p])
        for e in range(2):
            h = 2 * p + e
            logit = l2[:, e * qb:(e + 1) * qb] + madd
            if near:
                c0 = near_off - t0 + s0 + qb - LANES + sl
                logit = logit + jnp.concatenate(
                    [bias_ref[h, pl.ds(pl.multiple_of(c0 - g * LANES, LANES), sl), :]
                     for g in range(n_lane_tiles)], axis=1)
            logit = logit.astype(BF16)
            lg_scr[h] = logit
            part = jnp.max(logit.reshape(sl // (16 * FOLD_CHAINS), FOLD_CHAINS, 16, qb), axis=0)
            cm_scr[h] = jnp.max(jnp.max(part, axis=0).astype(F32), axis=0, keepdims=True)

    def softmax_head(j, h, buf):
        lg_scr, cm_scr = bufs[buf]
        m_prev = m_scr[h]
        m_new = jnp.maximum(m_prev, cm_scr[h])
        alpha = jnp.exp2(m_prev - m_new)
        pexp = jnp.exp2(lg_scr[h] - m_new.astype(BF16))
        acc_scr[h] = alpha * acc_scr[h] + _dot(vt_ref[j, h], pexp)
        m_scr[h] = m_new

    def logits_stage(j, buf, near):
        for p in range(n_pairs):
            logits_pair(j, p, buf, near)

    def softmax_stage(j, buf):
        for h in range(N_HEADS_A):
            softmax_head(j, h, buf)

    def overlapped(j_soft, buf_soft, j_logits, near):
        for p in range(n_pairs):
            logits_pair(j_logits, p, 1 - buf_soft, near)
            softmax_head(j_soft, 2 * p, buf_soft)
            softmax_head(j_soft, 2 * p + 1, buf_soft)

    def d_body(jj, carry, near):
        j = 2 * jj
        overlapped(j, 0, j + 1, near)
        overlapped(j + 1, 1, j + 2, near)
        return carry

    def odd_tail():
        softmax_stage(nslab - 1, 0)

    def even_tail():
        overlapped(nslab - 2, 0, nslab - 1, True)
        softmax_stage(nslab - 1, 1)

    lax.cond(jfar > 0, lambda: logits_stage(0, 0, False), lambda: logits_stage(0, 0, True))
    n_trips = (nslab - 1) // 2
    far_trips = jnp.minimum(jnp.maximum(jfar - 1, 0) // 2, n_trips)
    lax.fori_loop(0, far_trips, functools.partial(d_body, near=False), 0)
    lax.fori_loop(far_trips, n_trips, functools.partial(d_body, near=True), 0)
    lax.cond(lax.rem(nslab, 2) == 1, odd_tail, even_tail)

    for p in range(n_pairs):
        outs = [acc_scr[h, 0:HEAD_DIM_A] / acc_scr[h, HEAD_DIM_A:HEAD_DIM_A + 1] for h in (2 * p, 2 * p + 1)]
        o_ref[:, p * LANES:(p + 1) * LANES] = jnp.concatenate(outs, axis=0).T.astype(o_ref.dtype)


def _t5_bucket_np(rel):
    nb = NUM_BUCKETS // 2
    max_exact = nb // 2
    ret = np.where(rel > 0, nb, 0)
    n = np.abs(rel)
    nf = np.maximum(n, max_exact).astype(np.float64)
    large = max_exact + (np.log(nf / max_exact) / math.log(MAX_DISTANCE / max_exact) * (nb - max_exact)).astype(np.int64)
    large = np.minimum(large, nb - 1)
    return ret + np.where(n < max_exact, n, large)


def _t5_bucket(rel):
    nb = NUM_BUCKETS // 2
    max_exact = nb // 2
    ret = jnp.where(rel > 0, nb, 0)
    n = jnp.abs(rel)
    nf = jnp.maximum(n, max_exact).astype(F32)
    large = max_exact + (jnp.log(nf / max_exact) / math.log(MAX_DISTANCE / max_exact) * (nb - max_exact)).astype(jnp.int32)
    large = jnp.minimum(large, nb - 1)
    return ret + jnp.where(n < max_exact, n, large)


def _attention(qa, qi, wi, kb, vt, ki4, t5_bias, *, qb, sl, n_keys, pos0):
    bsz, tq, _ = qa.shape
    lp = kb.shape[1]
    assert tq % qb == 0 and lp % sl == 0 and sl % LANES == 0 and qb % LANES == 0
    ktop = min(TOPK_MAX, n_keys // 4)
    nslab_max = lp // sl

    gran = math.gcd(pos0, sl) if tq == qb else math.gcd(math.gcd(pos0, qb), sl)
    assert gran % LANES == 0
    rel_all = np.arange(-(n_keys + qb), 0)
    far_bucket = _t5_bucket_np(np.array([-(n_keys + qb)]))[0]
    sat = rel_all[_t5_bucket_np(rel_all) != far_bucket]
    n_sat = int(-sat.min()) + 1 if sat.size else 1
    near_off = -(-(sl - 1 - gran + n_sat) // gran) * gran
    rel_min = -(qb - 1) - near_off
    rels = jnp.arange(rel_min, sl, dtype=jnp.int32)
    tab = (t5_bias[_t5_bucket(rels)] - t5_bias[far_bucket][None, :]) * LOG2_E
    n_rel = sl - rel_min
    n_rows = near_off + sl + qb - LANES
    assert n_rows + LANES - 1 == n_rel
    hank = jnp.tile(tab.T, (1, LANES + 1))[:, :LANES * (n_rel + 1)].reshape(N_HEADS_A, LANES, n_rel + 1)
    master = jnp.transpose(hank[:, ::-1, :n_rows], (0, 2, 1))
    master = jnp.pad(master, ((0, 0), (sl, 0), (0, 0)))

    tri = jnp.asarray(np.tril(np.ones((sl, sl), np.float32)), dtype=BF16)

    kern = functools.partial(_attn_kernel, qb=qb, sl=sl, n_keys=n_keys, pos0=pos0,
                             near_off=near_off, ktop=float(ktop))

    def qspec(width_):
        return pl.BlockSpec((None, qb, width_), lambda b, i: (b, i, 0))

    def kspec(*shape):
        nd = len(shape)
        return pl.BlockSpec((None,) + shape, lambda b, i: (b,) + (0,) * nd, pipeline_mode=pl.Buffered(1))

    return pl.pallas_call(
        kern,
        grid=(bsz, tq // qb),
        in_specs=[qspec(W_A), qspec(W_IQ), qspec(LANES), kspec(lp, W_A),
                  kspec(nslab_max, N_HEADS_A, VT_ROWS, sl),
                  kspec(lp, W_IQ), _const_spec(master.shape), _const_spec((sl, sl))],
        out_specs=qspec(W_A),
        out_shape=jax.ShapeDtypeStruct((bsz, tq, W_A), BF16),
        scratch_shapes=[pltpu.VMEM((nslab_max, sl, qb), jnp.int32),
                        pltpu.VMEM((nslab_max + 1, sl, qb), I16),
                        pltpu.VMEM((N_HEADS_A // 2, LANES, 2 * qb), BF16),
                        pltpu.VMEM((N_HEADS_A, 1, qb), F32),
                        pltpu.VMEM((N_HEADS_A, VT_ROWS, qb), F32),
                        pltpu.VMEM((N_HEADS_A, sl, qb), BF16), pltpu.VMEM((N_HEADS_A, 1, qb), F32),
                        pltpu.VMEM((N_HEADS_A, sl, qb), BF16), pltpu.VMEM((N_HEADS_A, 1, qb), F32)],
        compiler_params=pltpu.CompilerParams(
            dimension_semantics=("arbitrary", "arbitrary"), vmem_limit_bytes=VMEM_LIMIT_BYTES),
        name="dsa_attention",
    )(qa, qi, wi, kb, vt, ki4, master, tri)


def _ret_kernel(q_ref, k_ref, v_ref, g_ref, gn_ref, s0_ref, dm_ref, qd_ref, kd_ref, gc_ref,
                r_ref, sfin_ref, s_scr, *, n_chunks):
    c = pl.program_id(1)

    @pl.when(c == 0)
    def _():
        s_scr[...] = s0_ref[...]

    for h in range(N_HEADS_R):
        p = h // 2
        q2 = q_ref[:, p * LANES:(p + 1) * LANES].astype(F32)
        k2 = k_ref[:, p * LANES:(p + 1) * LANES]
        vh = v_ref[:, h * VAL_DIM_R:(h + 1) * VAL_DIM_R]
        s_h = s_scr[h]
        qm = (q2 * qd_ref[h, 0]).astype(BF16)
        qdec = (q2 * qd_ref[h, 1]).astype(BF16)
        kdec = (k2.astype(F32) * kd_ref[h]).astype(BF16)
        inner = _dot_nt(qm, k2) * dm_ref[h]
        o = _dot(inner.astype(BF16), vh) + _dot(qdec, s_h.astype(BF16))
        s_scr[h] = gc_ref[h] * s_h + _dot_tn(kdec, vh)
        mu = jnp.mean(o, axis=-1, keepdims=True)
        d = o - mu
        var = jnp.mean(d * d, axis=-1, keepdims=True)
        on = d * lax.rsqrt(var + GN_EPS) * gn_ref[:, h * VAL_DIM_R:(h + 1) * VAL_DIM_R]
        gate = g_ref[:, h * VAL_DIM_R:(h + 1) * VAL_DIM_R].astype(F32)
        r_ref[:, h * VAL_DIM_R:(h + 1) * VAL_DIM_R] = (on * (gate * jax.nn.sigmoid(gate))).astype(r_ref.dtype)

    @pl.when(c == n_chunks - 1)
    def _():
        sfin_ref[...] = s_scr[...]


def _pair_lane_heads():
    return (np.arange(LANES) // (KEY_DIM_R // 2)) % 2


def _retention(qr, kr, vr, gr, gn_g, s_init, log_gamma, *, chunk):
    bsz, t, _ = qr.shape
    n_chunks = t // chunk
    n = jnp.arange(chunk, dtype=F32)
    lg = log_gamma.astype(F32)
    diff = n[:, None] - n[None, :]
    dmask = jnp.where(diff >= 0, jnp.exp(lg[:, None, None] * jnp.maximum(diff, 0.0)), 0.0)
    owner = jnp.asarray(_pair_lane_heads()[None, :] == (np.arange(N_HEADS_R) % 2)[:, None], F32)
    q_dec = jnp.exp(lg[:, None] * (n + 1.0))
    k_dec = jnp.exp(lg[:, None] * (chunk - 1.0 - n))
    qd = jnp.stack([jnp.broadcast_to(owner[:, None, :], (N_HEADS_R, chunk, LANES)),
                    owner[:, None, :] * q_dec[:, :, None]], axis=1)
    kd = owner[:, None, :] * k_dec[:, :, None]
    gc = jnp.broadcast_to(jnp.exp(lg * chunk)[:, None, None], (N_HEADS_R, 1, LANES))

    def tspec(width):
        return pl.BlockSpec((None, chunk, width), lambda b, c: (b, c, 0))

    sspec = pl.BlockSpec((None, N_HEADS_R, LANES, VAL_DIM_R), lambda b, c: (b, 0, 0, 0))
    return pl.pallas_call(
        functools.partial(_ret_kernel, n_chunks=n_chunks),
        grid=(bsz, n_chunks),
        in_specs=[tspec(W_RQK), tspec(W_RQK), tspec(W_RV), tspec(W_RV), _const_spec((1, W_RV)), sspec,
                  _const_spec(dmask.shape), _const_spec(qd.shape), _const_spec(kd.shape),
                  _const_spec(gc.shape)],
        out_specs=(tspec(W_RV), sspec),
        out_shape=(jax.ShapeDtypeStruct((bsz, t, W_RV), BF16),
                   jax.ShapeDtypeStruct((bsz, N_HEADS_R, LANES, VAL_DIM_R), F32)),
        scratch_shapes=[pltpu.VMEM((N_HEADS_R, LANES, VAL_DIM_R), F32)],
        compiler_params=pltpu.CompilerParams(
            dimension_semantics=("arbitrary", "arbitrary"), vmem_limit_bytes=VMEM_LIMIT_BYTES),
        name="retention",
    )(qr, kr, vr, gr, gn_g, s_init, dmask, qd, kd, gc)


def _finish_kernel(x_ref, a_ref, r_ref, ga_ref, gg_ref, wpa, wpr, wo, ln1g, ln1b, wg, wu, wd, ln2g, ln2b,
                   y_ref):
    x = x_ref[...]
    merged = (jax.nn.sigmoid(ga_ref[...].astype(F32)) * _dot(a_ref[...], wpa[...])
              + jax.nn.sigmoid(gg_ref[...].astype(F32)) * _dot(r_ref[...], wpr[...]))
    x1 = _layer_norm(ALPHA * x + _dot(merged.astype(BF16), wo[...]), ln1g[...], ln1b[...])
    x1b = x1.astype(BF16)
    gate = _dot(x1b, wg[...])
    hidden = gate * jax.nn.sigmoid(gate) * _dot(x1b, wu[...])
    y = _layer_norm(ALPHA * x1 + _dot(hidden.astype(BF16), wd[...]), ln2g[...], ln2b[...])
    y_ref[...] = y


def _finish(x2d, a, r, ga, gg, wpa, wpr, wo, ln1g, ln1b, wg, wu, wd, ln2g, ln2b, *, tm):
    m = x2d.shape[0]

    def rspec(width):
        return pl.BlockSpec((tm, width), lambda i: (i, 0))

    consts = (wpa, wpr, wo, ln1g, ln1b, wg, wu, wd, ln2g, ln2b)
    return pl.pallas_call(
        _finish_kernel,
        grid=(m // tm,),
        in_specs=[rspec(D_MODEL), rspec(W_A), rspec(W_RV), rspec(D_MODEL), rspec(D_MODEL)]
        + [_const_spec(c.shape) for c in consts],
        out_specs=rspec(D_MODEL),
        out_shape=jax.ShapeDtypeStruct((m, D_MODEL), F32),
        compiler_params=pltpu.CompilerParams(
            dimension_semantics=("arbitrary",), vmem_limit_bytes=VMEM_LIMIT_BYTES),
        name="finish",
    )(x2d, a, r, ga, gg, *consts)


def _relayout_w_in(w_in):
    offs = np.cumsum((0,) + SPLIT_SIZES)
    parts = [w_in[:, offs[i]:offs[i + 1]] for i in range(len(SPLIT_SIZES))]
    w_qa, w_ka, w_va, w_qi, w_ki, w_wi, w_qr, w_kr, w_vr, w_gr, w_ga, w_gg = parts
    half = KEY_DIM_R // 2
    perm = np.concatenate([
        np.arange(hh * KEY_DIM_R + part * half, hh * KEY_DIM_R + (part + 1) * half)
        for p in range(N_HEADS_R // 2) for part in (0, 1) for hh in (2 * p, 2 * p + 1)])
    w_wi_pad = jnp.pad(w_wi, ((0, 0), (HEAD_DIM_IDX, LANES - HEAD_DIM_IDX - N_HEADS_IDX)))
    w_all = jnp.concatenate(
        [w_qa, w_ka, w_va, w_qi, jnp.tile(w_ki, (1, N_HEADS_IDX)), w_wi_pad,
         w_qr[:, perm], w_kr[:, perm], w_vr, w_gr, w_ga, w_gg], axis=1)
    assert w_all.shape[1] == W_TOTAL
    return w_all.astype(BF16)


def _rotary_tables(pos):
    half = KEY_DIM_R // 2
    inv_freq = ROPE_BASE ** (-jnp.arange(half, dtype=F32) / half)
    ang = pos.astype(F32)[:, None] * inv_freq[None, :]
    cos, sin = jnp.cos(ang), jnp.sin(ang)
    return jnp.tile(cos, (1, 4)), jnp.concatenate([-sin, -sin, sin, sin], axis=1)


def _group(x, pos0, past_k, past_v, past_ki, state, w_all, g4, b4, t5_bias, log_gamma, fin_w,
           *, tm_proj, qb, sl, ret_chunk, tm_fin):
    bsz, t, _ = x.shape
    m = bsz * t
    x2d = x.reshape(m, D_MODEL)
    pos = pos0 + jnp.arange(t, dtype=jnp.int32)
    cos_t, sin_t = _rotary_tables(pos)
    n_pos_blocks = max(t // tm_proj, 1)
    if tm_proj > t:
        cos_t = jnp.tile(cos_t, (tm_proj // t, 1))
        sin_t = jnp.tile(sin_t, (tm_proj // t, 1))
    emit_vt = past_k is None and tm_proj == sl and t % sl == 0
    (qa, kf, kb, vf, vb, qi, kif, ki4, wi, qr, kr, vr, gr, ga, gg) = _proj(
        x2d, w_all, cos_t, sin_t, g4, b4, tm=tm_proj, n_pos_blocks=n_pos_blocks, emit_vt=emit_vt)

    def b3(a):
        return a.reshape(bsz, t, a.shape[-1])

    n_keys = t if past_k is None else past_k.shape[1] + t
    lp = -(-n_keys // sl) * sl
    kb3, ki43 = b3(kb), b3(ki4)
    if emit_vt:
        vt = vb.reshape(bsz, lp // sl, N_HEADS_A, VT_ROWS, sl)
    else:
        vb3 = b3(vb)
        if past_k is not None:
            pk = past_k.reshape(bsz, -1, W_A).astype(BF16)
            pv = past_v.reshape(bsz, -1, W_A).astype(BF16)
            pki = jnp.tile(past_ki, (1, 1, N_HEADS_IDX)).astype(BF16)
            kb3 = jnp.concatenate([pk, kb3], axis=1)
            vb3 = jnp.concatenate([pv, vb3], axis=1)
            ki43 = jnp.concatenate([pki, ki43], axis=1)
        if lp != n_keys:
            padw = ((0, 0), (0, lp - n_keys), (0, 0))
            kb3, vb3, ki43 = jnp.pad(kb3, padw), jnp.pad(vb3, padw), jnp.pad(ki43, padw)
        vt = vb3.reshape(bsz, lp // sl, sl, N_HEADS_A, HEAD_DIM_A).transpose(0, 1, 3, 4, 2)
        ones = jnp.ones((bsz, lp // sl, N_HEADS_A, VT_ROWS - HEAD_DIM_A, sl), BF16)
        vt = jnp.concatenate([vt, ones], axis=3)
    qa3, qi3, wi3 = b3(qa), b3(qi), b3(wi)
    tq = -(-t // qb) * qb
    if tq != t:
        padq = ((0, 0), (0, tq - t), (0, 0))
        qa3, qi3, wi3 = jnp.pad(qa3, padq), jnp.pad(qi3, padq), jnp.pad(wi3, padq)
    a = _attention(qa3, qi3, wi3, kb3, vt, ki43, t5_bias, qb=qb, sl=sl, n_keys=n_keys, pos0=pos0)[:, :t]

    half = KEY_DIM_R // 2
    npair = N_HEADS_R // 2
    if state is None:
        s_init = jnp.zeros((bsz, N_HEADS_R, LANES, VAL_DIM_R), F32)
    else:
        st = state.astype(F32).reshape(bsz, npair, 2, 2, half, VAL_DIM_R)
        zero = jnp.zeros_like(st[:, :, 0])
        s_init = jnp.stack([jnp.stack([st[:, :, 0], zero], axis=3),
                            jnp.stack([zero, st[:, :, 1]], axis=3)], axis=2)
        s_init = s_init.reshape(bsz, N_HEADS_R, LANES, VAL_DIM_R)
    r, s_fin = _retention(b3(qr), b3(kr), b3(vr), b3(gr), fin_w["gn"], s_init, log_gamma, chunk=ret_chunk)
    s6 = s_fin.reshape(bsz, npair, 2, 2, 2, half, VAL_DIM_R)
    s_out = jnp.stack([s6[:, :, 0, :, 0], s6[:, :, 1, :, 1]], axis=2)
    s_out = s_out.reshape(bsz, N_HEADS_R, KEY_DIM_R, VAL_DIM_R)

    y = _finish(x2d, a.reshape(m, W_A), r.reshape(m, W_RV), ga, gg,
                fin_w["wpa"], fin_w["wpr"], fin_w["wo"], fin_w["ln1g"], fin_w["ln1b"],
                fin_w["wg"], fin_w["wu"], fin_w["wd"], fin_w["ln2g"], fin_w["ln2b"], tm=tm_fin)
    return (y.reshape(bsz, t, D_MODEL),
            kf.reshape(bsz, t, N_HEADS_A, HEAD_DIM_A), vf.reshape(bsz, t, N_HEADS_A, HEAD_DIM_A),
            kif.reshape(bsz, t, HEAD_DIM_IDX), s_out)


def kernel(x_prompt, x_sample, cache_k, cache_v, cache_idx_k, state_ret, w_in, idx_k_norm_g, idx_k_norm_b,
           t5_bias, ret_gn_g, w_pa, w_pr, w_o, ln1_g, ln1_b, w_gate, w_up, w_down, ln2_g, ln2_b):
    assert w_in.shape[0] == DEPTH
    log_gamma = jnp.log1p(-jnp.exp2(-5.0 - jnp.arange(N_HEADS_R, dtype=F32)))
    l = 0
    w_all = _relayout_w_in(w_in[l])
    g4 = jnp.tile(idx_k_norm_g[l], N_HEADS_IDX)[None, :].astype(F32)
    b4 = jnp.tile(idx_k_norm_b[l], N_HEADS_IDX)[None, :].astype(F32)
    fin_w = dict(
        gn=ret_gn_g[l][None, :].astype(F32),
        wpa=w_pa[l].astype(BF16), wpr=w_pr[l].astype(BF16), wo=w_o[l].astype(BF16),
        ln1g=ln1_g[l][None, :].astype(F32), ln1b=ln1_b[l][None, :].astype(F32),
        wg=w_gate[l].astype(BF16), wu=w_up[l].astype(BF16), wd=w_down[l].astype(BF16),
        ln2g=ln2_g[l][None, :].astype(F32), ln2b=ln2_b[l][None, :].astype(F32))

    t_p = x_prompt.shape[1]
    yp, kp, vp, ikp, sp = _group(
        x_prompt, 0, None, None, None, None, w_all, g4, b4, t5_bias, log_gamma, fin_w,
        tm_proj=min(512, t_p), qb=min(256, t_p), sl=512, ret_chunk=min(256, t_p), tm_fin=min(256, t_p))

    bs, t_s, _ = x_sample.shape
    past = cache_k.shape[2]
    ys, ks, vs, iks, ss = _group(
        x_sample, past, cache_k[l], cache_v[l], cache_idx_k[l], state_ret[l], w_all, g4, b4, t5_bias,
        log_gamma, fin_w, tm_proj=bs * t_s, qb=LANES, sl=512, ret_chunk=t_s, tm_fin=bs * t_s)

    return (yp, ys, kp[None], vp[None], ikp[None], sp[None],
            ks[None], vs[None], iks[None], ss.astype(state_ret.dtype)[None])
```

```python
import functools
import math

import numpy as np
import jax
import jax.numpy as jnp
from jax import lax
from jax.experimental import pallas as pl
from jax.experimental.pallas import tpu as pltpu

D_MODEL = 1024
CHUNK = 64
N_HEADS_A = 8
HEAD_DIM_A = 64
N_HEADS_IDX = 4
HEAD_DIM_IDX = 64
TOPK_MAX = 256
NUM_BUCKETS = 32
MAX_DISTANCE = 1024
N_HEADS_R = 8
KEY_DIM_R = 64
VAL_DIM_R = 128
ROPE_BASE = 10000.0
D_FF = 2816
DEPTH = 1
ALPHA = (2.0 * DEPTH) ** 0.25
LN_EPS = 1e-5
GN_EPS = 1e-6

W_A = N_HEADS_A * HEAD_DIM_A
W_IQ = N_HEADS_IDX * HEAD_DIM_IDX
W_RQK = N_HEADS_R * KEY_DIM_R
W_RV = N_HEADS_R * VAL_DIM_R
SPLIT_SIZES = (W_A, W_A, W_A, W_IQ, HEAD_DIM_IDX, N_HEADS_IDX, W_RQK, W_RQK, W_RV, W_RV, D_MODEL, D_MODEL)

LANES = 128
VMEM_LIMIT_BYTES = 60 * 1024 * 1024

OFF_QA = 0
OFF_KA = OFF_QA + W_A
OFF_VA = OFF_KA + W_A
OFF_QI = OFF_VA + W_A
OFF_KI4 = OFF_QI + W_IQ
OFF_WI = OFF_KI4 + W_IQ
OFF_QR = OFF_WI + LANES
OFF_KR = OFF_QR + W_RQK
OFF_VR = OFF_KR + W_RQK
OFF_GR = OFF_VR + W_RV
OFF_GA = OFF_GR + W_RV
OFF_GG = OFF_GA + D_MODEL
W_TOTAL = OFF_GG + D_MODEL

INT_MIN = -(2 ** 31)
NEG_BIG = -1e30
LOG2_E = math.log2(math.e)
FOLD_CHAINS = 8
VT_ROWS = HEAD_DIM_A + 16
BF16 = jnp.bfloat16
F32 = jnp.float32
I16 = jnp.int16
ONE16, ZERO16 = np.int16(1), np.int16(0)
MAX16, MIN16 = np.int16(2 ** 15 - 1), np.int16(-(2 ** 15))


def _dot(a, b):
    return jnp.dot(a, b, preferred_element_type=F32)


def _dot_nt(a, b):
    return lax.dot_general(a, b, (((1,), (1,)), ((), ())), preferred_element_type=F32)


def _dot_tn(a, b):
    return lax.dot_general(a, b, (((0,), (0,)), ((), ())), preferred_element_type=F32)


def _layer_norm(z, g, b):
    mu = jnp.mean(z, axis=-1, keepdims=True)
    d = z - mu
    var = jnp.mean(d * d, axis=-1, keepdims=True)
    return d * lax.rsqrt(var + LN_EPS) * g + b


def _const_spec(shape):
    nd = len(shape)
    return pl.BlockSpec(shape, lambda *_: (0,) * nd, pipeline_mode=pl.Buffered(1))


def _proj_kernel(x_ref, w_ref, cos_ref, sin_ref, g4_ref, b4_ref,
                 qa_o, kf_o, kb_o, vf_o, vb_o, qi_o, kif_o, ki4_o, wi_o,
                 qr_o, kr_o, vr_o, gr_o, ga_o, gg_o, *, emit_vt):
    xb = x_ref[...].astype(BF16)

    def seg(off, n):
        return _dot(xb, w_ref[:, off:off + n])

    qa_o[...] = (seg(OFF_QA, W_A) * (HEAD_DIM_A ** -0.5 * LOG2_E)).astype(BF16)
    k = seg(OFF_KA, W_A)
    kf_o[...] = k.reshape(kf_o.shape)
    kb_o[...] = k.astype(BF16)
    v = seg(OFF_VA, W_A)
    vf_o[...] = v.reshape(vf_o.shape)
    if emit_vt:
        v_t = v.T
        for h in range(N_HEADS_A):
            vb_o[h, 0:HEAD_DIM_A, :] = v_t[h * HEAD_DIM_A:(h + 1) * HEAD_DIM_A].astype(BF16)
            vb_o[h, HEAD_DIM_A:VT_ROWS, :] = jnp.ones((VT_ROWS - HEAD_DIM_A, v_t.shape[1]), BF16)
    else:
        vb_o[...] = v.astype(BF16)
    qi_o[...] = seg(OFF_QI, W_IQ).astype(BF16)

    ki = seg(OFF_KI4, W_IQ)
    first = lax.broadcasted_iota(jnp.int32, ki.shape, 1) < HEAD_DIM_IDX
    inv_n = 1.0 / HEAD_DIM_IDX
    mu = jnp.sum(jnp.where(first, ki, 0.0), axis=-1, keepdims=True) * inv_n
    d = ki - mu
    var = jnp.sum(jnp.where(first, d * d, 0.0), axis=-1, keepdims=True) * inv_n
    kin = d * lax.rsqrt(var + LN_EPS) * g4_ref[...] + b4_ref[...]
    kif_o[...] = kin[:, :HEAD_DIM_IDX]
    ki4_o[...] = kin.astype(BF16)

    wi_o[...] = seg(OFF_WI, LANES) * ((N_HEADS_IDX ** -0.5) * (HEAD_DIM_IDX ** -0.5))

    for off, o_ref, scale in ((OFF_QR, qr_o, 1.0), (OFF_KR, kr_o, KEY_DIM_R ** -0.5)):
        h = seg(off, W_RQK)
        for j in range(W_RQK // LANES):
            xj = h[:, j * LANES:(j + 1) * LANES]
            rj = pltpu.roll(xj, LANES // 2, 1)
            oj = xj * cos_ref[...] + rj * sin_ref[...]
            o_ref[:, j * LANES:(j + 1) * LANES] = (oj * scale).astype(BF16)

    vr_o[...] = seg(OFF_VR, W_RV).astype(BF16)
    gr_o[...] = seg(OFF_GR, W_RV).astype(BF16)
    ga_o[...] = seg(OFF_GA, D_MODEL).astype(BF16)
    gg_o[...] = seg(OFF_GG, D_MODEL).astype(BF16)


def _proj(x2d, w_all, cos_t, sin_t, g4, b4, *, tm, n_pos_blocks, emit_vt):
    m = x2d.shape[0]
    grid = (m // tm,)

    def row(width, dtype):
        return jax.ShapeDtypeStruct((m, width), dtype)

    def rspec(width):
        return pl.BlockSpec((tm, width), lambda i: (i, 0))

    pos_spec = pl.BlockSpec((tm, LANES), lambda i: (i % n_pos_blocks, 0))
    out_shapes = [
        row(W_A, BF16), row(W_A, F32), row(W_A, BF16), row(W_A, F32), row(W_A, BF16),
        row(W_IQ, BF16), row(HEAD_DIM_IDX, F32), row(W_IQ, BF16), row(LANES, F32),
        row(W_RQK, BF16), row(W_RQK, BF16), row(W_RV, BF16), row(W_RV, BF16),
        row(D_MODEL, BF16), row(D_MODEL, BF16)]
    out_specs = [rspec(s.shape[1]) for s in out_shapes]
    for idx in (1, 3):
        out_shapes[idx] = jax.ShapeDtypeStruct((m, N_HEADS_A, HEAD_DIM_A), F32)
        out_specs[idx] = pl.BlockSpec((tm, N_HEADS_A, HEAD_DIM_A), lambda i: (i, 0, 0))
    if emit_vt:
        out_shapes[4] = jax.ShapeDtypeStruct((m // tm, N_HEADS_A, VT_ROWS, tm), BF16)
        out_specs[4] = pl.BlockSpec((None, N_HEADS_A, VT_ROWS, tm), lambda i: (i, 0, 0, 0))
    return pl.pallas_call(
        functools.partial(_proj_kernel, emit_vt=emit_vt),
        grid=grid,
        in_specs=[rspec(D_MODEL), _const_spec((D_MODEL, W_TOTAL)), pos_spec, pos_spec,
                  _const_spec((1, W_IQ)), _const_spec((1, W_IQ))],
        out_specs=tuple(out_specs),
        out_shape=tuple(out_shapes),
        compiler_params=pltpu.CompilerParams(
            dimension_semantics=("arbitrary",), vmem_limit_bytes=VMEM_LIMIT_BYTES),
        name="proj",
    )(x2d, w_all, cos_t, sin_t, g4, b4)


def _attn_kernel(qa_ref, qi_ref, wi_ref, kb_ref, vt_ref, ki_ref, bias_ref, tri_ref, o_ref,
                 key_scr, half_scr, qt_scr, m_scr, acc_scr, lga_scr, cma_scr, lgb_scr, cmb_scr, tie_scr,
                 *, qb, sl, n_keys, pos0, near_off, ktop):
    i = pl.program_id(1)
    t0 = pos0 + i * qb
    lblk = jnp.minimum(n_keys, ((t0 + qb - 1) // CHUNK + 1) * CHUNK)
    nslab = (lblk + sl - 1) // sl
    n_lane_tiles = qb // LANES

    qi_t = qi_ref[...].astype(F32).T
    qi_h = [qi_t[h * HEAD_DIM_IDX:(h + 1) * HEAD_DIM_IDX].astype(BF16) for h in range(N_HEADS_IDX)]
    w_t = wi_ref[...].T
    w_h = [w_t[HEAD_DIM_IDX + h:HEAD_DIM_IDX + h + 1] for h in range(N_HEADS_IDX)]
    qpos = t0 + lax.broadcasted_iota(jnp.int32, (1, qb), 1)
    qlim = jnp.minimum((lax.shift_right_logical(qpos, int(math.log2(CHUNK))) + 1) * CHUNK, n_keys)
    key_row = lax.broadcasted_iota(jnp.int32, (sl, qb), 0)

    def a_body(j, carry, masked):
        s0 = pl.multiple_of(j * sl, sl)
        kis = ki_ref[pl.ds(s0, sl), 0:HEAD_DIM_IDX]
        s = w_h[0] * jnp.maximum(_dot(kis, qi_h[0]), 0.0)
        for h in range(1, N_HEADS_IDX):
            s = s + w_h[h] * jnp.maximum(_dot(kis, qi_h[h]), 0.0)
        b = lax.bitcast_convert_type(s, jnp.int32)
        skey = jnp.where(b < 0, INT_MIN - b, b)
        if masked:
            skey = jnp.where(s0 + key_row < qlim, skey, INT_MIN)
        key_scr[j] = skey
        half_scr[j] = lax.shift_right_arithmetic(skey, 16).astype(I16)
        return carry

    n_open = jnp.minimum(jnp.minimum((t0 // CHUNK + 1) * CHUNK, n_keys) // sl, nslab)
    lax.fori_loop(0, n_open, functools.partial(a_body, masked=False), 0)
    lax.fori_loop(n_open, nslab, functools.partial(a_body, masked=True), 0)
    half_scr[nslab] = jnp.full((sl, qb), MIN16, I16)

    def count_ge16(cand16):
        def c_body(jj, acc):
            for j in (2 * jj, 2 * jj + 1):
                hit = jnp.where(half_scr[j] >= cand16, ONE16, ZERO16)
                hit = hit.reshape(sl // (16 * FOLD_CHAINS), FOLD_CHAINS, 16, qb)
                for a in range(hit.shape[0]):
                    acc = acc + hit[a]
            return acc
        acc = lax.fori_loop(0, (nslab + 1) // 2, c_body, jnp.zeros((FOLD_CHAINS, 16, qb), I16))
        return jnp.sum(acc.astype(F32).reshape(FOLD_CHAINS * 16, qb), axis=0, keepdims=True)

    def radix16(n_above):
        def bit_body(it, carry):
            t_u, n_above = carry
            cand_u = t_u | jnp.left_shift(jnp.int32(1), 15 - it)
            cnt = count_ge16((cand_u - 2 ** 15).astype(I16))
            take = cnt >= ktop
            return jnp.where(take, cand_u, t_u), jnp.where(take, n_above, cnt)
        return lax.fori_loop(0, 16, bit_body, (jnp.zeros((1, qb), jnp.int32), n_above))

    hi_u, n_above_hi = radix16(jnp.zeros((1, qb), F32))
    hi16 = (hi_u - 2 ** 15).astype(I16)

    def low_body(j, carry):
        low = ((key_scr[j] & 0xFFFF) - 2 ** 15).astype(I16)
        high = half_scr[j]
        half_scr[j] = jnp.where(high == hi16, low, jnp.where(high > hi16, MAX16, MIN16))
        return carry

    lax.fori_loop(0, nslab, low_body, 0)
    low_u, n_gt = radix16(n_above_hi)
    thr = (hi_u - 2 ** 15) * 2 ** 16 + low_u
    need = jnp.where(thr == INT_MIN, 0.0, ktop - n_gt)

    tie_scr[...] = jnp.zeros(tie_scr.shape, F32)

    def mask_slab(j):
        keys = key_scr[j]
        eq = keys == thr
        pref = _dot(tri_ref[...], jnp.where(eq, 1.0, 0.0).astype(BF16))
        seen = tie_scr[...]
        sel = (keys > thr) | (eq & (pref + seen <= need))
        key_scr[j] = lax.bitcast_convert_type(jnp.where(sel, 0.0, NEG_BIG), jnp.int32)
        tie_scr[...] = seen + pref[sl - 1:sl]

    jfar = jnp.minimum(jnp.maximum(t0 - near_off + sl - 1, 0) // sl, nslab)
    n_pairs = N_HEADS_A // 2
    qa_t = qa_ref[...].astype(F32).T
    low = lax.broadcasted_iota(jnp.int32, (LANES, qb), 0) < HEAD_DIM_A
    for p in range(n_pairs):
        blk = qa_t[p * LANES:(p + 1) * LANES]
        qt_scr[p, :, 0:qb] = jnp.where(low, blk, 0.0).astype(BF16)
        qt_scr[p, :, qb:2 * qb] = jnp.where(low, 0.0, blk).astype(BF16)
    m_scr[...] = jnp.full(m_scr.shape, NEG_BIG, F32)
    acc_scr[...] = jnp.zeros(acc_scr.shape, F32)

    bufs = ((lga_scr, cma_scr), (lgb_scr, cmb_scr))

    def logits_pair(j, p, buf, near):
        lg_scr, cm_scr = bufs[buf]
        s0 = pl.multiple_of(j * sl, sl)
        madd = lax.bitcast_convert_type(key_scr[j], F32)
        k2 = kb_ref[pl.ds(s0, sl), p * LANES:(p + 1) * LANES]
        l2 = _dot(k2, qt_scr[p])
        for e in range(2):
            h = 2 * p + e
            logit = l2[:, e * qb:(e + 1) * qb] + madd
            if near:
                c0 = near_off - t0 + s0 + qb - LANES + sl
                logit = logit + jnp.concatenate(
                    [bias_ref[h, pl.ds(pl.multiple_of(c0 - g * LANES, LANES), sl), :]
                     for g in range(n_lane_tiles)], axis=1)
            logit = logit.astype(BF16)
            lg_scr[h] = logit
            part = jnp.max(logit.reshape(sl // (16 * FOLD_CHAINS), FOLD_CHAINS, 16, qb), axis=0)
            cm_scr[h] = jnp.max(jnp.max(part, axis=0).astype(F32), axis=0, keepdims=True)

    def softmax_head(j, h, buf):
        lg_scr, cm_scr = bufs[buf]
        m_prev = m_scr[h]
        m_new = jnp.maximum(m_prev, cm_scr[h])
        alpha = jnp.exp2(m_prev - m_new)
        pexp = jnp.exp2(lg_scr[h] - m_new.astype(BF16))
        acc_scr[h] = alpha * acc_scr[h] + _dot(vt_ref[j, h], pexp)
        m_scr[h] = m_new

    def logits_stage(j, buf, near):
        for p in range(n_pairs):
            logits_pair(j, p, buf, near)

    def softmax_stage(j, buf):
        for h in range(N_HEADS_A):
            softmax_head(j, h, buf)

    def overlapped(j_soft, buf_soft, j_logits, near):
        mask_slab(j_logits)
        for p in range(n_pairs):
            logits_pair(j_logits, p, 1 - buf_soft, near)
            softmax_head(j_soft, 2 * p, buf_soft)
            softmax_head(j_soft, 2 * p + 1, buf_soft)

    def d_body(jj, carry, near):
        j = 2 * jj
        overlapped(j, 0, j + 1, near)
        overlapped(j + 1, 1, j + 2, near)
        return carry

    def odd_tail():
        softmax_stage(nslab - 1, 0)

    def even_tail():
        overlapped(nslab - 2, 0, nslab - 1, True)
        softmax_stage(nslab - 1, 1)

    mask_slab(0)
    lax.cond(jfar > 0, lambda: logits_stage(0, 0, False), lambda: logits_stage(0, 0, True))
    n_trips = (nslab - 1) // 2
    far_trips = jnp.minimum(jnp.maximum(jfar - 1, 0) // 2, n_trips)
    lax.fori_loop(0, far_trips, functools.partial(d_body, near=False), 0)
    lax.fori_loop(far_trips, n_trips, functools.partial(d_body, near=True), 0)
    lax.cond(lax.rem(nslab, 2) == 1, odd_tail, even_tail)

    for p in range(n_pairs):
        outs = [acc_scr[h, 0:HEAD_DIM_A] / acc_scr[h, HEAD_DIM_A:HEAD_DIM_A + 1] for h in (2 * p, 2 * p + 1)]
        o_ref[:, p * LANES:(p + 1) * LANES] = jnp.concatenate(outs, axis=0).T.astype(o_ref.dtype)


def _t5_bucket_np(rel):
    nb = NUM_BUCKETS // 2
    max_exact = nb // 2
    ret = np.where(rel > 0, nb, 0)
    n = np.abs(rel)
    nf = np.maximum(n, max_exact).astype(np.float64)
    large = max_exact + (np.log(nf / max_exact) / math.log(MAX_DISTANCE / max_exact) * (nb - max_exact)).astype(np.int64)
    large = np.minimum(large, nb - 1)
    return ret + np.where(n < max_exact, n, large)


def _t5_bucket(rel):
    nb = NUM_BUCKETS // 2
    max_exact = nb // 2
    ret = jnp.where(rel > 0, nb, 0)
    n = jnp.abs(rel)
    nf = jnp.maximum(n, max_exact).astype(F32)
    large = max_exact + (jnp.log(nf / max_exact) / math.log(MAX_DISTANCE / max_exact) * (nb - max_exact)).astype(jnp.int32)
    large = jnp.minimum(large, nb - 1)
    return ret + jnp.where(n < max_exact, n, large)


def _attention(qa, qi, wi, kb, vt, ki4, t5_bias, *, qb, sl, n_keys, pos0):
    bsz, tq, _ = qa.shape
    lp = kb.shape[1]
    assert tq % qb == 0 and lp % sl == 0 and sl % LANES == 0 and qb % LANES == 0
    ktop = min(TOPK_MAX, n_keys // 4)
    nslab_max = lp // sl

    gran = math.gcd(pos0, sl) if tq == qb else math.gcd(math.gcd(pos0, qb), sl)
    assert gran % LANES == 0
    rel_all = np.arange(-(n_keys + qb), 0)
    far_bucket = _t5_bucket_np(np.array([-(n_keys + qb)]))[0]
    sat = rel_all[_t5_bucket_np(rel_all) != far_bucket]
    n_sat = int(-sat.min()) + 1 if sat.size else 1
    near_off = -(-(sl - 1 - gran + n_sat) // gran) * gran
    rel_min = -(qb - 1) - near_off
    rels = jnp.arange(rel_min, sl, dtype=jnp.int32)
    tab = (t5_bias[_t5_bucket(rels)] - t5_bias[far_bucket][None, :]) * LOG2_E
    n_rel = sl - rel_min
    n_rows = near_off + sl + qb - LANES
    assert n_rows + LANES - 1 == n_rel
    rolled = jnp.roll(tab.T, -(LANES - 1), axis=1)
    toep = jnp.tile(rolled, (1, LANES))[:, :LANES * (n_rel - 1)].reshape(N_HEADS_A, LANES, n_rel - 1)
    master = jnp.transpose(toep[:, :, :n_rows], (0, 2, 1))
    master = jnp.pad(master, ((0, 0), (sl, 0), (0, 0)))

    tri = jnp.asarray(np.tril(np.ones((sl, sl), np.float32)), dtype=BF16)

    kern = functools.partial(_attn_kernel, qb=qb, sl=sl, n_keys=n_keys, pos0=pos0,
                             near_off=near_off, ktop=float(ktop))

    def qspec(width_):
        return pl.BlockSpec((None, qb, width_), lambda b, i: (b, i, 0))

    def kspec(*shape):
        nd = len(shape)
        return pl.BlockSpec((None,) + shape, lambda b, i: (b,) + (0,) * nd, pipeline_mode=pl.Buffered(1))

    return pl.pallas_call(
        kern,
        grid=(bsz, tq // qb),
        in_specs=[qspec(W_A), qspec(W_IQ), qspec(LANES), kspec(lp, W_A),
                  kspec(nslab_max, N_HEADS_A, VT_ROWS, sl),
                  kspec(lp, W_IQ), _const_spec(master.shape), _const_spec((sl, sl))],
        out_specs=qspec(W_A),
        out_shape=jax.ShapeDtypeStruct((bsz, tq, W_A), BF16),
        scratch_shapes=[pltpu.VMEM((nslab_max, sl, qb), jnp.int32),
                        pltpu.VMEM((nslab_max + 1, sl, qb), I16),
                        pltpu.VMEM((N_HEADS_A // 2, LANES, 2 * qb), BF16),
                        pltpu.VMEM((N_HEADS_A, 1, qb), F32),
                        pltpu.VMEM((N_HEADS_A, VT_ROWS, qb), F32),
                        pltpu.VMEM((N_HEADS_A, sl, qb), BF16), pltpu.VMEM((N_HEADS_A, 1, qb), F32),
                        pltpu.VMEM((N_HEADS_A, sl, qb), BF16), pltpu.VMEM((N_HEADS_A, 1, qb), F32),
                        pltpu.VMEM((1, qb), F32)],
        compiler_params=pltpu.CompilerParams(
            dimension_semantics=("arbitrary", "arbitrary"), vmem_limit_bytes=VMEM_LIMIT_BYTES),
        name="dsa_attention",
    )(qa, qi, wi, kb, vt, ki4, master, tri)


def _ret_kernel(q_ref, k_ref, v_ref, g_ref, gn_ref, s0_ref, dm_ref, qd_ref, kd_ref, gc_ref,
                r_ref, sfin_ref, s_scr, *, n_chunks):
    c = pl.program_id(1)

    @pl.when(c == 0)
    def _():
        s_scr[...] = s0_ref[...]

    for h in range(N_HEADS_R):
        p = h // 2
        q2 = q_ref[:, p * LANES:(p + 1) * LANES].astype(F32)
        k2 = k_ref[:, p * LANES:(p + 1) * LANES]
        vh = v_ref[:, h * VAL_DIM_R:(h + 1) * VAL_DIM_R]
        s_h = s_scr[h]
        qm = (q2 * qd_ref[h, 0]).astype(BF16)
        qdec = (q2 * qd_ref[h, 1]).astype(BF16)
        kdec = (k2.astype(F32) * kd_ref[h]).astype(BF16)
        inner = _dot_nt(qm, k2) * dm_ref[h]
        o = _dot(inner.astype(BF16), vh) + _dot(qdec, s_h.astype(BF16))
        s_scr[h] = gc_ref[h] * s_h + _dot_tn(kdec, vh)
        mu = jnp.mean(o, axis=-1, keepdims=True)
        d = o - mu
        var = jnp.mean(d * d, axis=-1, keepdims=True)
        on = d * lax.rsqrt(var + GN_EPS) * gn_ref[:, h * VAL_DIM_R:(h + 1) * VAL_DIM_R]
        gate = g_ref[:, h * VAL_DIM_R:(h + 1) * VAL_DIM_R].astype(F32)
        r_ref[:, h * VAL_DIM_R:(h + 1) * VAL_DIM_R] = (on * (gate * jax.nn.sigmoid(gate))).astype(r_ref.dtype)

    @pl.when(c == n_chunks - 1)
    def _():
        sfin_ref[...] = s_scr[...]


def _pair_lane_heads():
    return (np.arange(LANES) // (KEY_DIM_R // 2)) % 2


def _retention(qr, kr, vr, gr, gn_g, s_init, log_gamma, *, chunk):
    bsz, t, _ = qr.shape
    n_chunks = t // chunk
    n = jnp.arange(chunk, dtype=F32)
    lg = log_gamma.astype(F32)
    diff = n[:, None] - n[None, :]
    dmask = jnp.where(diff >= 0, jnp.exp(lg[:, None, None] * jnp.maximum(diff, 0.0)), 0.0)
    owner = jnp.asarray(_pair_lane_heads()[None, :] == (np.arange(N_HEADS_R) % 2)[:, None], F32)
    q_dec = jnp.exp(lg[:, None] * (n + 1.0))
    k_dec = jnp.exp(lg[:, None] * (chunk - 1.0 - n))
    qd = jnp.stack([jnp.broadcast_to(owner[:, None, :], (N_HEADS_R, chunk, LANES)),
                    owner[:, None, :] * q_dec[:, :, None]], axis=1)
    kd = owner[:, None, :] * k_dec[:, :, None]
    gc = jnp.broadcast_to(jnp.exp(lg * chunk)[:, None, None], (N_HEADS_R, 1, LANES))

    def tspec(width):
        return pl.BlockSpec((None, chunk, width), lambda b, c: (b, c, 0))

    sspec = pl.BlockSpec((None, N_HEADS_R, LANES, VAL_DIM_R), lambda b, c: (b, 0, 0, 0))
    return pl.pallas_call(
        functools.partial(_ret_kernel, n_chunks=n_chunks),
        grid=(bsz, n_chunks),
        in_specs=[tspec(W_RQK), tspec(W_RQK), tspec(W_RV), tspec(W_RV), _const_spec((1, W_RV)), sspec,
                  _const_spec(dmask.shape), _const_spec(qd.shape), _const_spec(kd.shape),
                  _const_spec(gc.shape)],
        out_specs=(tspec(W_RV), sspec),
        out_shape=(jax.ShapeDtypeStruct((bsz, t, W_RV), BF16),
                   jax.ShapeDtypeStruct((bsz, N_HEADS_R, LANES, VAL_DIM_R), F32)),
        scratch_shapes=[pltpu.VMEM((N_HEADS_R, LANES, VAL_DIM_R), F32)],
        compiler_params=pltpu.CompilerParams(
            dimension_semantics=("arbitrary", "arbitrary"), vmem_limit_bytes=VMEM_LIMIT_BYTES),
        name="retention",
    )(qr, kr, vr, gr, gn_g, s_init, dmask, qd, kd, gc)


def _finish_kernel(x_ref, a_ref, r_ref, ga_ref, gg_ref, wpa, wpr, wo, ln1g, ln1b, wg, wu, wd, ln2g, ln2b,
                   y_ref):
    x = x_ref[...]
    merged = (jax.nn.sigmoid(ga_ref[...].astype(F32)) * _dot(a_ref[...], wpa[...])
              + jax.nn.sigmoid(gg_ref[...].astype(F32)) * _dot(r_ref[...], wpr[...]))
    x1 = _layer_norm(ALPHA * x + _dot(merged.astype(BF16), wo[...]), ln1g[...], ln1b[...])
    x1b = x1.astype(BF16)
    gate = _dot(x1b, wg[...])
    hidden = gate * jax.nn.sigmoid(gate) * _dot(x1b, wu[...])
    y = _layer_norm(ALPHA * x1 + _dot(hidden.astype(BF16), wd[...]), ln2g[...], ln2b[...])
    y_ref[...] = y


def _finish(x2d, a, r, ga, gg, wpa, wpr, wo, ln1g, ln1b, wg, wu, wd, ln2g, ln2b, *, tm):
    m = x2d.shape[0]

    def rspec(width):
        return pl.BlockSpec((tm, width), lambda i: (i, 0))

    consts = (wpa, wpr, wo, ln1g, ln1b, wg, wu, wd, ln2g, ln2b)
    return pl.pallas_call(
        _finish_kernel,
        grid=(m // tm,),
        in_specs=[rspec(D_MODEL), rspec(W_A), rspec(W_RV), rspec(D_MODEL), rspec(D_MODEL)]
        + [_const_spec(c.shape) for c in consts],
        out_specs=rspec(D_MODEL),
        out_shape=jax.ShapeDtypeStruct((m, D_MODEL), F32),
        compiler_params=pltpu.CompilerParams(
            dimension_semantics=("arbitrary",), vmem_limit_bytes=VMEM_LIMIT_BYTES),
        name="finish",
    )(x2d, a, r, ga, gg, *consts)


def _relayout_w_in(w_in):
    offs = np.cumsum((0,) + SPLIT_SIZES)
    parts = [w_in[:, offs[i]:offs[i + 1]] for i in range(len(SPLIT_SIZES))]
    w_qa, w_ka, w_va, w_qi, w_ki, w_wi, w_qr, w_kr, w_vr, w_gr, w_ga, w_gg = parts
    half = KEY_DIM_R // 2
    perm = np.concatenate([
        np.arange(hh * KEY_DIM_R + part * half, hh * KEY_DIM_R + (part + 1) * half)
        for p in range(N_HEADS_R // 2) for part in (0, 1) for hh in (2 * p, 2 * p + 1)])
    w_wi_pad = jnp.pad(w_wi, ((0, 0), (HEAD_DIM_IDX, LANES - HEAD_DIM_IDX - N_HEADS_IDX)))
    w_all = jnp.concatenate(
        [w_qa, w_ka, w_va, w_qi, jnp.tile(w_ki, (1, N_HEADS_IDX)), w_wi_pad,
         w_qr[:, perm], w_kr[:, perm], w_vr, w_gr, w_ga, w_gg], axis=1)
    assert w_all.shape[1] == W_TOTAL
    return w_all.astype(BF16)


def _rotary_tables(pos):
    half = KEY_DIM_R // 2
    inv_freq = ROPE_BASE ** (-jnp.arange(half, dtype=F32) / half)
    ang = pos.astype(F32)[:, None] * inv_freq[None, :]
    cos, sin = jnp.cos(ang), jnp.sin(ang)
    return jnp.tile(cos, (1, 4)), jnp.concatenate([-sin, -sin, sin, sin], axis=1)


def _group(x, pos0, past_k, past_v, past_ki, state, w_all, g4, b4, t5_bias, log_gamma, fin_w,
           *, tm_proj, qb, sl, ret_chunk, tm_fin):
    bsz, t, _ = x.shape
    m = bsz * t
    x2d = x.reshape(m, D_MODEL)
    pos = pos0 + jnp.arange(t, dtype=jnp.int32)
    cos_t, sin_t = _rotary_tables(pos)
    n_pos_blocks = max(t // tm_proj, 1)
    if tm_proj > t:
        cos_t = jnp.tile(cos_t, (tm_proj // t, 1))
        sin_t = jnp.tile(sin_t, (tm_proj // t, 1))
    emit_vt = past_k is None and tm_proj == sl and t % sl == 0
    (qa, kf, kb, vf, vb, qi, kif, ki4, wi, qr, kr, vr, gr, ga, gg) = _proj(
        x2d, w_all, cos_t, sin_t, g4, b4, tm=tm_proj, n_pos_blocks=n_pos_blocks, emit_vt=emit_vt)

    def b3(a):
        return a.reshape(bsz, t, a.shape[-1])

    n_keys = t if past_k is None else past_k.shape[1] + t
    lp = -(-n_keys // sl) * sl
    kb3, ki43 = b3(kb), b3(ki4)
    if emit_vt:
        vt = vb.reshape(bsz, lp // sl, N_HEADS_A, VT_ROWS, sl)
    else:
        vb3 = b3(vb)
        if past_k is not None:
            pk = past_k.reshape(bsz, -1, W_A).astype(BF16)
            pv = past_v.reshape(bsz, -1, W_A).astype(BF16)
            pki = jnp.tile(past_ki, (1, 1, N_HEADS_IDX)).astype(BF16)
            kb3 = jnp.concatenate([pk, kb3], axis=1)
            vb3 = jnp.concatenate([pv, vb3], axis=1)
            ki43 = jnp.concatenate([pki, ki43], axis=1)
        if lp != n_keys:
            padw = ((0, 0), (0, lp - n_keys), (0, 0))
            kb3, vb3, ki43 = jnp.pad(kb3, padw), jnp.pad(vb3, padw), jnp.pad(ki43, padw)
        vt = vb3.reshape(bsz, lp // sl, sl, W_A).transpose(0, 1, 3, 2)
        vt = vt.reshape(bsz, lp // sl, N_HEADS_A, HEAD_DIM_A, sl)
        vt = jnp.pad(vt, ((0, 0), (0, 0), (0, 0), (0, VT_ROWS - HEAD_DIM_A), (0, 0)), constant_values=1)
    qa3, qi3, wi3 = b3(qa), b3(qi), b3(wi)
    tq = -(-t // qb) * qb
    if tq != t:
        padq = ((0, 0), (0, tq - t), (0, 0))
        qa3, qi3, wi3 = jnp.pad(qa3, padq), jnp.pad(qi3, padq), jnp.pad(wi3, padq)
    a = _attention(qa3, qi3, wi3, kb3, vt, ki43, t5_bias, qb=qb, sl=sl, n_keys=n_keys, pos0=pos0)[:, :t]

    half = KEY_DIM_R // 2
    npair = N_HEADS_R // 2
    if state is None:
        s_init = jnp.zeros((bsz, N_HEADS_R, LANES, VAL_DIM_R), F32)
    else:
        st = state.astype(F32).reshape(bsz, npair, 2, 2, half, VAL_DIM_R)
        zero = jnp.zeros_like(st[:, :, 0])
        s_init = jnp.stack([jnp.stack([st[:, :, 0], zero], axis=3),
                            jnp.stack([zero, st[:, :, 1]], axis=3)], axis=2)
        s_init = s_init.reshape(bsz, N_HEADS_R, LANES, VAL_DIM_R)
    r, s_fin = _retention(b3(qr), b3(kr), b3(vr), b3(gr), fin_w["gn"], s_init, log_gamma, chunk=ret_chunk)
    s6 = s_fin.reshape(bsz, npair, 2, 2, 2, half, VAL_DIM_R)
    s_out = jnp.stack([s6[:, :, 0, :, 0], s6[:, :, 1, :, 1]], axis=2)
    s_out = s_out.reshape(bsz, N_HEADS_R, KEY_DIM_R, VAL_DIM_R)

    y = _finish(x2d, a.reshape(m, W_A), r.reshape(m, W_RV), ga, gg,
                fin_w["wpa"], fin_w["wpr"], fin_w["wo"], fin_w["ln1g"], fin_w["ln1b"],
                fin_w["wg"], fin_w["wu"], fin_w["wd"], fin_w["ln2g"], fin_w["ln2b"], tm=tm_fin)
    return (y.reshape(bsz, t, D_MODEL),
            kf.reshape(bsz, t, N_HEADS_A, HEAD_DIM_A), vf.reshape(bsz, t, N_HEADS_A, HEAD_DIM_A),
            kif.reshape(bsz, t, HEAD_DIM_IDX), s_out)


def kernel(x_prompt, x_sample, cache_k, cache_v, cache_idx_k, state_ret, w_in, idx_k_norm_g, idx_k_norm_b,
           t5_bias, ret_gn_g, w_pa, w_pr, w_o, ln1_g, ln1_b, w_gate, w_up, w_down, ln2_g, ln2_b):
    assert w_in.shape[0] == DEPTH
    log_gamma = jnp.log1p(-jnp.exp2(-5.0 - jnp.arange(N_HEADS_R, dtype=F32)))
    l = 0
    w_all = _relayout_w_in(w_in[l])
    g4 = jnp.tile(idx_k_norm_g[l], N_HEADS_IDX)[None, :].astype(F32)
    b4 = jnp.tile(idx_k_norm_b[l], N_HEADS_IDX)[None, :].astype(F32)
    fin_w = dict(
        gn=ret_gn_g[l][None, :].astype(F32),
        wpa=w_pa[l].astype(BF16), wpr=w_pr[l].astype(BF16), wo=w_o[l].astype(BF16),
        ln1g=ln1_g[l][None, :].astype(F32), ln1b=ln1_b[l][None, :].astype(F32),
        wg=w_gate[l].astype(BF16), wu=w_up[l].astype(BF16), wd=w_down[l].astype(BF16),
        ln2g=ln2_g[l][None, :].astype(F32), ln2b=ln2_b[l][None, :].astype(F32))

    t_p = x_prompt.shape[1]
    yp, kp, vp, ikp, sp = _group(
        x_prompt, 0, None, None, None, None, w_all, g4, b4, t5_bias, log_gamma, fin_w,
        tm_proj=min(512, t_p), qb=min(256, t_p), sl=512, ret_chunk=min(256, t_p), tm_fin=min(512, t_p))

    bs, t_s, _ = x_sample.shape
    past = cache_k.shape[2]
    ys, ks, vs, iks, ss = _group(
        x_sample, past, cache_k[l], cache_v[l], cache_idx_k[l], state_ret[l], w_all, g4, b4, t5_bias,
        log_gamma, fin_w, tm_proj=bs * t_s, qb=LANES, sl=512, ret_chunk=t_s, tm_fin=bs * t_s)

    return (yp, ys, kp[None], vp[None], ikp[None], sp[None],
            ks[None], vs[None], iks[None], ss.astype(state_ret.dtype)[None])
```

```python
import functools
import math

import numpy as np
import jax
import jax.numpy as jnp
from jax import lax
from jax.experimental import pallas as pl
from jax.experimental.pallas import tpu as pltpu

D_MODEL = 1024
CHUNK = 64
N_HEADS_A = 8
HEAD_DIM_A = 64
N_HEADS_IDX = 4
HEAD_DIM_IDX = 64
TOPK_MAX = 256
NUM_BUCKETS = 32
MAX_DISTANCE = 1024
N_HEADS_R = 8
KEY_DIM_R = 64
VAL_DIM_R = 128
ROPE_BASE = 10000.0
D_FF = 2816
DEPTH = 1
ALPHA = (2.0 * DEPTH) ** 0.25
LN_EPS = 1e-5
GN_EPS = 1e-6

W_A = N_HEADS_A * HEAD_DIM_A
W_IQ = N_HEADS_IDX * HEAD_DIM_IDX
W_RQK = N_HEADS_R * KEY_DIM_R
W_RV = N_HEADS_R * VAL_DIM_R
SPLIT_SIZES = (W_A, W_A, W_A, W_IQ, HEAD_DIM_IDX, N_HEADS_IDX, W_RQK, W_RQK, W_RV, W_RV, D_MODEL, D_MODEL)

LANES = 128
VMEM_LIMIT_BYTES = 60 * 1024 * 1024

OFF_QA = 0
OFF_KA = OFF_QA + W_A
OFF_VA = OFF_KA + W_A
OFF_QI = OFF_VA + W_A
OFF_KW = OFF_QI + W_IQ
OFF_QR = OFF_KW + LANES
OFF_KR = OFF_QR + W_RQK
OFF_VR = OFF_KR + W_RQK
OFF_GR = OFF_VR + W_RV
OFF_GA = OFF_GR + W_RV
OFF_GG = OFF_GA + D_MODEL
W_TOTAL = OFF_GG + D_MODEL

INT_MIN = -(2 ** 31)
NEG_BIG = -1e30
LOG2_E = math.log2(math.e)
FOLD_CHAINS = 8
VT_ROWS = HEAD_DIM_A + 16
BF16 = jnp.bfloat16
F32 = jnp.float32
I16 = jnp.int16
ONE16, ZERO16 = np.int16(1), np.int16(0)
MAX16, MIN16 = np.int16(2 ** 15 - 1), np.int16(-(2 ** 15))


def _dot(a, b):
    return jnp.dot(a, b, preferred_element_type=F32)


def _dot_nt(a, b):
    return lax.dot_general(a, b, (((1,), (1,)), ((), ())), preferred_element_type=F32)


def _dot_tn(a, b):
    return lax.dot_general(a, b, (((0,), (0,)), ((), ())), preferred_element_type=F32)


def _layer_norm(z, g, b):
    mu = jnp.mean(z, axis=-1, keepdims=True)
    d = z - mu
    var = jnp.mean(d * d, axis=-1, keepdims=True)
    return d * lax.rsqrt(var + LN_EPS) * g + b


def _const_spec(shape):
    nd = len(shape)
    return pl.BlockSpec(shape, lambda *_: (0,) * nd, pipeline_mode=pl.Buffered(1))


def _proj_kernel(x_ref, w_ref, cos_ref, sin_ref, g_ref, b_ref,
                 qa_o, kf_o, kb_o, vf_o, vb_o, qi_o, kif_o, kib_o, wi_o,
                 qr_o, kr_o, vr_o, gr_o, ga_o, gg_o, *, emit_vt):
    xb = x_ref[...].astype(BF16)

    def seg(off, n):
        return _dot(xb, w_ref[:, off:off + n])

    qa_o[...] = (seg(OFF_QA, W_A) * (HEAD_DIM_A ** -0.5 * LOG2_E)).astype(BF16)
    k = seg(OFF_KA, W_A)
    kf_o[...] = k.reshape(kf_o.shape)
    kb_o[...] = k.astype(BF16)
    v = seg(OFF_VA, W_A)
    vf_o[...] = v.reshape(vf_o.shape)
    if emit_vt:
        v_t = v.T
        for h in range(N_HEADS_A):
            vb_o[h, 0:HEAD_DIM_A, :] = v_t[h * HEAD_DIM_A:(h + 1) * HEAD_DIM_A].astype(BF16)
            vb_o[h, HEAD_DIM_A:VT_ROWS, :] = jnp.ones((VT_ROWS - HEAD_DIM_A, v_t.shape[1]), BF16)
    else:
        vb_o[...] = v.astype(BF16)
    qi_o[...] = seg(OFF_QI, W_IQ).astype(BF16)

    kw = seg(OFF_KW, LANES)
    first = lax.broadcasted_iota(jnp.int32, kw.shape, 1) < HEAD_DIM_IDX
    inv_n = 1.0 / HEAD_DIM_IDX
    mu = jnp.sum(jnp.where(first, kw, 0.0), axis=-1, keepdims=True) * inv_n
    d = kw - mu
    var = jnp.sum(jnp.where(first, d * d, 0.0), axis=-1, keepdims=True) * inv_n
    kin = d * lax.rsqrt(var + LN_EPS) * g_ref[...] + b_ref[...]
    kif_o[...] = kin[:, :HEAD_DIM_IDX]
    kib_o[...] = kin.astype(BF16)

    wi_o[...] = kw * ((N_HEADS_IDX ** -0.5) * (HEAD_DIM_IDX ** -0.5))

    for off, o_ref, scale in ((OFF_QR, qr_o, 1.0), (OFF_KR, kr_o, KEY_DIM_R ** -0.5)):
        h = seg(off, W_RQK)
        for j in range(W_RQK // LANES):
            xj = h[:, j * LANES:(j + 1) * LANES]
            rj = pltpu.roll(xj, LANES // 2, 1)
            oj = xj * cos_ref[...] + rj * sin_ref[...]
            o_ref[:, j * LANES:(j + 1) * LANES] = (oj * scale).astype(BF16)

    vr_o[...] = seg(OFF_VR, W_RV).astype(BF16)
    gr_o[...] = seg(OFF_GR, W_RV).astype(BF16)
    ga_o[...] = seg(OFF_GA, D_MODEL).astype(BF16)
    gg_o[...] = seg(OFF_GG, D_MODEL).astype(BF16)


def _proj(x2d, w_all, cos_t, sin_t, g_pad, b_pad, *, tm, n_pos_blocks, emit_vt):
    m = x2d.shape[0]
    grid = (m // tm,)

    def row(width, dtype):
        return jax.ShapeDtypeStruct((m, width), dtype)

    def rspec(width):
        return pl.BlockSpec((tm, width), lambda i: (i, 0))

    pos_spec = pl.BlockSpec((tm, LANES), lambda i: (i % n_pos_blocks, 0))
    out_shapes = [
        row(W_A, BF16), row(W_A, F32), row(W_A, BF16), row(W_A, F32), row(W_A, BF16),
        row(W_IQ, BF16), row(HEAD_DIM_IDX, F32), row(LANES, BF16), row(LANES, F32),
        row(W_RQK, BF16), row(W_RQK, BF16), row(W_RV, BF16), row(W_RV, BF16),
        row(D_MODEL, BF16), row(D_MODEL, BF16)]
    out_specs = [rspec(s.shape[1]) for s in out_shapes]
    for idx in (1, 3):
        out_shapes[idx] = jax.ShapeDtypeStruct((m, N_HEADS_A, HEAD_DIM_A), F32)
        out_specs[idx] = pl.BlockSpec((tm, N_HEADS_A, HEAD_DIM_A), lambda i: (i, 0, 0))
    if emit_vt:
        out_shapes[4] = jax.ShapeDtypeStruct((m // tm, N_HEADS_A, VT_ROWS, tm), BF16)
        out_specs[4] = pl.BlockSpec((None, N_HEADS_A, VT_ROWS, tm), lambda i: (i, 0, 0, 0))
    return pl.pallas_call(
        functools.partial(_proj_kernel, emit_vt=emit_vt),
        grid=grid,
        in_specs=[rspec(D_MODEL), _const_spec((D_MODEL, W_TOTAL)), pos_spec, pos_spec,
                  _const_spec((1, LANES)), _const_spec((1, LANES))],
        out_specs=tuple(out_specs),
        out_shape=tuple(out_shapes),
        compiler_params=pltpu.CompilerParams(
            dimension_semantics=("arbitrary",), vmem_limit_bytes=VMEM_LIMIT_BYTES),
        name="proj",
    )(x2d, w_all, cos_t, sin_t, g_pad, b_pad)


def _attn_kernel(qa_ref, qi_ref, wi_ref, kb_ref, vt_ref, ki_ref, bias_ref, tri_ref, o_ref,
                 key_scr, half_scr, qt_scr, m_scr, acc_scr, lga_scr, cma_scr, lgb_scr, cmb_scr, tie_scr,
                 *, qb, sl, n_keys, pos0, near_off, ktop):
    i = pl.program_id(1)
    t0 = pos0 + i * qb
    lblk = jnp.minimum(n_keys, ((t0 + qb - 1) // CHUNK + 1) * CHUNK)
    nslab = (lblk + sl - 1) // sl
    n_lane_tiles = qb // LANES

    qi_t = qi_ref[...].astype(F32).T
    qi_h = [qi_t[h * HEAD_DIM_IDX:(h + 1) * HEAD_DIM_IDX].astype(BF16) for h in range(N_HEADS_IDX)]
    w_t = wi_ref[...].T
    w_h = [w_t[HEAD_DIM_IDX + h:HEAD_DIM_IDX + h + 1] for h in range(N_HEADS_IDX)]
    qpos = t0 + lax.broadcasted_iota(jnp.int32, (1, qb), 1)
    qlim = jnp.minimum((lax.shift_right_logical(qpos, int(math.log2(CHUNK))) + 1) * CHUNK, n_keys)
    key_row = lax.broadcasted_iota(jnp.int32, (sl, qb), 0)

    def a_body(j, carry, masked):
        s0 = pl.multiple_of(j * sl, sl)
        kis = ki_ref[pl.ds(s0, sl), 0:HEAD_DIM_IDX]
        s = w_h[0] * jnp.maximum(_dot(kis, qi_h[0]), 0.0)
        for h in range(1, N_HEADS_IDX):
            s = s + w_h[h] * jnp.maximum(_dot(kis, qi_h[h]), 0.0)
        b = lax.bitcast_convert_type(s, jnp.int32)
        skey = jnp.where(b < 0, INT_MIN - b, b)
        if masked:
            skey = jnp.where(s0 + key_row < qlim, skey, INT_MIN)
        key_scr[j] = skey
        half_scr[j] = lax.shift_right_arithmetic(skey, 16).astype(I16)
        return carry

    n_open = jnp.minimum(jnp.minimum((t0 // CHUNK + 1) * CHUNK, n_keys) // sl, nslab)

    def a_pair(jj, carry):
        a_body(2 * jj, carry, False)
        return a_body(2 * jj + 1, carry, False)

    lax.fori_loop(0, n_open // 2, a_pair, 0)
    lax.fori_loop(2 * (n_open // 2), nslab, functools.partial(a_body, masked=True), 0)
    half_scr[nslab] = jnp.full((sl, qb), MIN16, I16)

    def count_ge16(cand16):
        def c_body(jj, acc):
            for j in (2 * jj, 2 * jj + 1):
                hit = jnp.where(half_scr[j] >= cand16, ONE16, ZERO16)
                hit = hit.reshape(sl // (16 * FOLD_CHAINS), FOLD_CHAINS, 16, qb)
                for a in range(hit.shape[0]):
                    acc = acc + hit[a]
            return acc
        acc = lax.fori_loop(0, (nslab + 1) // 2, c_body, jnp.zeros((FOLD_CHAINS, 16, qb), I16))
        return jnp.sum(acc.astype(F32).reshape(FOLD_CHAINS * 16, qb), axis=0, keepdims=True)

    def radix16(n_above):
        def bit_body(it, carry):
            t_u, n_above = carry
            cand_u = t_u | jnp.left_shift(jnp.int32(1), 15 - it)
            cnt = count_ge16((cand_u - 2 ** 15).astype(I16))
            take = cnt >= ktop
            return jnp.where(take, cand_u, t_u), jnp.where(take, n_above, cnt)
        return lax.fori_loop(0, 16, bit_body, (jnp.zeros((1, qb), jnp.int32), n_above))

    hi_u, n_above_hi = radix16(jnp.zeros((1, qb), F32))
    hi16 = (hi_u - 2 ** 15).astype(I16)

    def low_body(j, carry):
        low = ((key_scr[j] & 0xFFFF) - 2 ** 15).astype(I16)
        high = half_scr[j]
        half_scr[j] = jnp.where(high == hi16, low, jnp.where(high > hi16, MAX16, MIN16))
        return carry

    lax.fori_loop(0, nslab, low_body, 0)
    low_u, n_gt = radix16(n_above_hi)
    thr = (hi_u - 2 ** 15) * 2 ** 16 + low_u
    need = jnp.where(thr == INT_MIN, 0.0, ktop - n_gt)

    tie_scr[...] = jnp.zeros(tie_scr.shape, F32)

    def mask_slab(j):
        keys = key_scr[j]
        eq = keys == thr
        pref = _dot(tri_ref[...], jnp.where(eq, 1.0, 0.0).astype(BF16))
        seen = tie_scr[...]
        sel = (keys > thr) | (eq & (pref + seen <= need))
        key_scr[j] = lax.bitcast_convert_type(jnp.where(sel, 0.0, NEG_BIG), jnp.int32)
        tie_scr[...] = seen + pref[sl - 1:sl]

    jfar = jnp.minimum(jnp.maximum(t0 - near_off + sl - 1, 0) // sl, nslab)
    n_pairs = N_HEADS_A // 2
    qa_t = qa_ref[...].astype(F32).T
    low = lax.broadcasted_iota(jnp.int32, (LANES, qb), 0) < HEAD_DIM_A
    for p in range(n_pairs):
        blk = qa_t[p * LANES:(p + 1) * LANES]
        qt_scr[p, :, 0:qb] = jnp.where(low, blk, 0.0).astype(BF16)
        qt_scr[p, :, qb:2 * qb] = jnp.where(low, 0.0, blk).astype(BF16)
    m_scr[...] = jnp.full(m_scr.shape, NEG_BIG, F32)
    acc_scr[...] = jnp.zeros(acc_scr.shape, F32)

    bufs = ((lga_scr, cma_scr), (lgb_scr, cmb_scr))

    def logits_pair(j, p, buf, near):
        lg_scr, cm_scr = bufs[buf]
        s0 = pl.multiple_of(j * sl, sl)
        madd = lax.bitcast_convert_type(key_scr[j], F32)
        k2 = kb_ref[pl.ds(s0, sl), p * LANES:(p + 1) * LANES]
        l2 = _dot(k2, qt_scr[p])
        for e in range(2):
            h = 2 * p + e
            logit = l2[:, e * qb:(e + 1) * qb] + madd
            if near:
                c0 = near_off - t0 + s0 + qb - LANES + sl
                logit = logit + jnp.concatenate(
                    [bias_ref[h, pl.ds(pl.multiple_of(c0 - g * LANES, LANES), sl), :]
                     for g in range(n_lane_tiles)], axis=1)
            logit = logit.astype(BF16)
            lg_scr[h] = logit
            part = jnp.max(logit.reshape(sl // (16 * FOLD_CHAINS), FOLD_CHAINS, 16, qb), axis=0)
            cm_scr[h] = jnp.max(jnp.max(part, axis=0).astype(F32), axis=0, keepdims=True)

    def softmax_head(j, h, buf):
        lg_scr, cm_scr = bufs[buf]
        m_prev = m_scr[h]
        m_new = jnp.maximum(m_prev, cm_scr[h])
        alpha = jnp.exp2(m_prev - m_new)
        pexp = jnp.exp2(lg_scr[h] - m_new.astype(BF16))
        acc_scr[h] = alpha * acc_scr[h] + _dot(vt_ref[j, h], pexp)
        m_scr[h] = m_new

    def logits_stage(j, buf, near):
        for p in range(n_pairs):
            logits_pair(j, p, buf, near)

    def softmax_stage(j, buf):
        for h in range(N_HEADS_A):
            softmax_head(j, h, buf)

    def overlapped(j_soft, buf_soft, j_logits, near):
        mask_slab(j_logits)
        for p in range(n_pairs):
            logits_pair(j_logits, p, 1 - buf_soft, near)
            softmax_head(j_soft, 2 * p, buf_soft)
            softmax_head(j_soft, 2 * p + 1, buf_soft)

    def d_body(jj, carry, near):
        j = 2 * jj
        overlapped(j, 0, j + 1, near)
        overlapped(j + 1, 1, j + 2, near)
        return carry

    def odd_tail():
        softmax_stage(nslab - 1, 0)

    def even_tail():
        overlapped(nslab - 2, 0, nslab - 1, True)
        softmax_stage(nslab - 1, 1)

    mask_slab(0)
    lax.cond(jfar > 0, lambda: logits_stage(0, 0, False), lambda: logits_stage(0, 0, True))
    n_trips = (nslab - 1) // 2
    far_trips = jnp.minimum(jnp.maximum(jfar - 1, 0) // 2, n_trips)
    lax.fori_loop(0, far_trips, functools.partial(d_body, near=False), 0)
    lax.fori_loop(far_trips, n_trips, functools.partial(d_body, near=True), 0)
    lax.cond(lax.rem(nslab, 2) == 1, odd_tail, even_tail)

    for p in range(n_pairs):
        outs = [acc_scr[h, 0:HEAD_DIM_A] / acc_scr[h, HEAD_DIM_A:HEAD_DIM_A + 1] for h in (2 * p, 2 * p + 1)]
        o_ref[:, p * LANES:(p + 1) * LANES] = jnp.concatenate(outs, axis=0).T.astype(o_ref.dtype)


def _t5_bucket_np(rel):
    nb = NUM_BUCKETS // 2
    max_exact = nb // 2
    ret = np.where(rel > 0, nb, 0)
    n = np.abs(rel)
    nf = np.maximum(n, max_exact).astype(np.float64)
    large = max_exact + (np.log(nf / max_exact) / math.log(MAX_DISTANCE / max_exact) * (nb - max_exact)).astype(np.int64)
    large = np.minimum(large, nb - 1)
    return ret + np.where(n < max_exact, n, large)


def _t5_bucket(rel):
    nb = NUM_BUCKETS // 2
    max_exact = nb // 2
    ret = jnp.where(rel > 0, nb, 0)
    n = jnp.abs(rel)
    nf = jnp.maximum(n, max_exact).astype(F32)
    large = max_exact + (jnp.log(nf / max_exact) / math.log(MAX_DISTANCE / max_exact) * (nb - max_exact)).astype(jnp.int32)
    large = jnp.minimum(large, nb - 1)
    return ret + jnp.where(n < max_exact, n, large)


def _attention(qa, qi, wi, kb, vt, kib, t5_bias, *, qb, sl, n_keys, pos0):
    bsz, tq, _ = qa.shape
    lp = kb.shape[1]
    assert tq % qb == 0 and lp % sl == 0 and sl % LANES == 0 and qb % LANES == 0
    ktop = min(TOPK_MAX, n_keys // 4)
    nslab_max = lp // sl

    gran = math.gcd(pos0, sl) if tq == qb else math.gcd(math.gcd(pos0, qb), sl)
    assert gran % LANES == 0
    rel_all = np.arange(-(n_keys + qb), 0)
    far_bucket = _t5_bucket_np(np.array([-(n_keys + qb)]))[0]
    sat = rel_all[_t5_bucket_np(rel_all) != far_bucket]
    n_sat = int(-sat.min()) + 1 if sat.size else 1
    near_off = -(-(sl - 1 - gran + n_sat) // gran) * gran
    rel_min = -(qb - 1) - near_off
    rels = jnp.arange(rel_min, sl, dtype=jnp.int32)
    tab = (t5_bias[_t5_bucket(rels)] - t5_bias[far_bucket][None, :]) * LOG2_E
    n_rel = sl - rel_min
    n_rows = near_off + sl + qb - LANES
    assert n_rows + LANES - 1 == n_rel
    rolled = jnp.roll(tab.T, -(LANES - 1), axis=1)
    toep = jnp.tile(rolled, (1, LANES))[:, :LANES * (n_rel - 1)].reshape(N_HEADS_A, LANES, n_rel - 1)
    master = jnp.transpose(toep[:, :, :n_rows], (0, 2, 1))
    master = jnp.pad(master, ((0, 0), (sl, 0), (0, 0)))

    tri = jnp.asarray(np.tril(np.ones((sl, sl), np.float32)), dtype=BF16)

    kern = functools.partial(_attn_kernel, qb=qb, sl=sl, n_keys=n_keys, pos0=pos0,
                             near_off=near_off, ktop=float(ktop))

    def qspec(width_):
        return pl.BlockSpec((None, qb, width_), lambda b, i: (b, i, 0))

    def kspec(*shape):
        nd = len(shape)
        return pl.BlockSpec((None,) + shape, lambda b, i: (b,) + (0,) * nd, pipeline_mode=pl.Buffered(1))

    return pl.pallas_call(
        kern,
        grid=(bsz, tq // qb),
        in_specs=[qspec(W_A), qspec(W_IQ), qspec(LANES), kspec(lp, W_A),
                  kspec(nslab_max, N_HEADS_A, VT_ROWS, sl),
                  kspec(lp, LANES), _const_spec(master.shape), _const_spec((sl, sl))],
        out_specs=qspec(W_A),
        out_shape=jax.ShapeDtypeStruct((bsz, tq, W_A), BF16),
        scratch_shapes=[pltpu.VMEM((nslab_max, sl, qb), jnp.int32),
                        pltpu.VMEM((nslab_max + 1, sl, qb), I16),
                        pltpu.VMEM((N_HEADS_A // 2, LANES, 2 * qb), BF16),
                        pltpu.VMEM((N_HEADS_A, 1, qb), F32),
                        pltpu.VMEM((N_HEADS_A, VT_ROWS, qb), F32),
                        pltpu.VMEM((N_HEADS_A, sl, qb), BF16), pltpu.VMEM((N_HEADS_A, 1, qb), F32),
                        pltpu.VMEM((N_HEADS_A, sl, qb), BF16), pltpu.VMEM((N_HEADS_A, 1, qb), F32),
                        pltpu.VMEM((1, qb), F32)],
        compiler_params=pltpu.CompilerParams(
            dimension_semantics=("arbitrary", "arbitrary"), vmem_limit_bytes=VMEM_LIMIT_BYTES),
        name="dsa_attention",
    )(qa, qi, wi, kb, vt, kib, master, tri)


def _ret_kernel(q_ref, k_ref, v_ref, g_ref, gn_ref, s0_ref, dm_ref, qd_ref, kd_ref, gc_ref,
                r_ref, sfin_ref, s_scr, *, n_chunks):
    c = pl.program_id(1)

    @pl.when(c == 0)
    def _():
        s_scr[...] = s0_ref[...]

    for h in range(N_HEADS_R):
        p = h // 2
        q2 = q_ref[:, p * LANES:(p + 1) * LANES].astype(F32)
        k2 = k_ref[:, p * LANES:(p + 1) * LANES]
        vh = v_ref[:, h * VAL_DIM_R:(h + 1) * VAL_DIM_R]
        s_h = s_scr[h]
        qm = (q2 * qd_ref[h, 0]).astype(BF16)
        qdec = (q2 * qd_ref[h, 1]).astype(BF16)
        kdec = (k2.astype(F32) * kd_ref[h]).astype(BF16)
        inner = _dot_nt(qm, k2) * dm_ref[h]
        o = _dot(inner.astype(BF16), vh) + _dot(qdec, s_h.astype(BF16))
        s_scr[h] = gc_ref[h] * s_h + _dot_tn(kdec, vh)
        mu = jnp.mean(o, axis=-1, keepdims=True)
        d = o - mu
        var = jnp.mean(d * d, axis=-1, keepdims=True)
        on = d * lax.rsqrt(var + GN_EPS) * gn_ref[:, h * VAL_DIM_R:(h + 1) * VAL_DIM_R]
        gate = g_ref[:, h * VAL_DIM_R:(h + 1) * VAL_DIM_R].astype(F32)
        r_ref[:, h * VAL_DIM_R:(h + 1) * VAL_DIM_R] = (on * (gate * jax.nn.sigmoid(gate))).astype(r_ref.dtype)

    @pl.when(c == n_chunks - 1)
    def _():
        sfin_ref[...] = s_scr[...]


def _pair_lane_heads():
    return (np.arange(LANES) // (KEY_DIM_R // 2)) % 2


def _retention(qr, kr, vr, gr, gn_g, s_init, log_gamma, *, chunk):
    bsz, t, _ = qr.shape
    n_chunks = t // chunk
    n = jnp.arange(chunk, dtype=F32)
    lg = log_gamma.astype(F32)
    diff = n[:, None] - n[None, :]
    dmask = jnp.where(diff >= 0, jnp.exp(lg[:, None, None] * jnp.maximum(diff, 0.0)), 0.0)
    owner = jnp.asarray(_pair_lane_heads()[None, :] == (np.arange(N_HEADS_R) % 2)[:, None], F32)
    q_dec = jnp.exp(lg[:, None] * (n + 1.0))
    k_dec = jnp.exp(lg[:, None] * (chunk - 1.0 - n))
    qd = jnp.stack([jnp.broadcast_to(owner[:, None, :], (N_HEADS_R, chunk, LANES)),
                    owner[:, None, :] * q_dec[:, :, None]], axis=1)
    kd = owner[:, None, :] * k_dec[:, :, None]
    gc = jnp.broadcast_to(jnp.exp(lg * chunk)[:, None, None], (N_HEADS_R, 1, LANES))

    def tspec(width):
        return pl.BlockSpec((None, chunk, width), lambda b, c: (b, c, 0))

    sspec = pl.BlockSpec((None, N_HEADS_R, LANES, VAL_DIM_R), lambda b, c: (b, 0, 0, 0))
    return pl.pallas_call(
        functools.partial(_ret_kernel, n_chunks=n_chunks),
        grid=(bsz, n_chunks),
        in_specs=[tspec(W_RQK), tspec(W_RQK), tspec(W_RV), tspec(W_RV), _const_spec((1, W_RV)), sspec,
                  _const_spec(dmask.shape), _const_spec(qd.shape), _const_spec(kd.shape),
                  _const_spec(gc.shape)],
        out_specs=(tspec(W_RV), sspec),
        out_shape=(jax.ShapeDtypeStruct((bsz, t, W_RV), BF16),
                   jax.ShapeDtypeStruct((bsz, N_HEADS_R, LANES, VAL_DIM_R), F32)),
        scratch_shapes=[pltpu.VMEM((N_HEADS_R, LANES, VAL_DIM_R), F32)],
        compiler_params=pltpu.CompilerParams(
            dimension_semantics=("arbitrary", "arbitrary"), vmem_limit_bytes=VMEM_LIMIT_BYTES),
        name="retention",
    )(qr, kr, vr, gr, gn_g, s_init, dmask, qd, kd, gc)


def _finish_kernel(x_ref, a_ref, r_ref, ga_ref, gg_ref, wpa, wpr, wo, ln1g, ln1b, wg, wu, wd, ln2g, ln2b,
                   y_ref):
    x = x_ref[...]
    merged = (jax.nn.sigmoid(ga_ref[...].astype(F32)) * _dot(a_ref[...], wpa[...])
              + jax.nn.sigmoid(gg_ref[...].astype(F32)) * _dot(r_ref[...], wpr[...]))
    x1 = _layer_norm(ALPHA * x + _dot(merged.astype(BF16), wo[...]), ln1g[...], ln1b[...])
    x1b = x1.astype(BF16)
    gate = _dot(x1b, wg[...])
    hidden = gate * jax.nn.sigmoid(gate) * _dot(x1b, wu[...])
    y = _layer_norm(ALPHA * x1 + _dot(hidden.astype(BF16), wd[...]), ln2g[...], ln2b[...])
    y_ref[...] = y


def _finish(x2d, a, r, ga, gg, wpa, wpr, wo, ln1g, ln1b, wg, wu, wd, ln2g, ln2b, *, tm):
    m = x2d.shape[0]

    def rspec(width):
        return pl.BlockSpec((tm, width), lambda i: (i, 0))

    consts = (wpa, wpr, wo, ln1g, ln1b, wg, wu, wd, ln2g, ln2b)
    return pl.pallas_call(
        _finish_kernel,
        grid=(m // tm,),
        in_specs=[rspec(D_MODEL), rspec(W_A), rspec(W_RV), rspec(D_MODEL), rspec(D_MODEL)]
        + [_const_spec(c.shape) for c in consts],
        out_specs=rspec(D_MODEL),
        out_shape=jax.ShapeDtypeStruct((m, D_MODEL), F32),
        compiler_params=pltpu.CompilerParams(
            dimension_semantics=("arbitrary",), vmem_limit_bytes=VMEM_LIMIT_BYTES),
        name="finish",
    )(x2d, a, r, ga, gg, *consts)


def _relayout_w_in(w_in):
    offs = np.cumsum((0,) + SPLIT_SIZES)
    parts = [w_in[:, offs[i]:offs[i + 1]] for i in range(len(SPLIT_SIZES))]
    w_qa, w_ka, w_va, w_qi, w_ki, w_wi, w_qr, w_kr, w_vr, w_gr, w_ga, w_gg = parts
    half = KEY_DIM_R // 2
    perm = np.concatenate([
        np.arange(hh * KEY_DIM_R + part * half, hh * KEY_DIM_R + (part + 1) * half)
        for p in range(N_HEADS_R // 2) for part in (0, 1) for hh in (2 * p, 2 * p + 1)])
    w_kw = jnp.pad(jnp.concatenate([w_ki, w_wi], axis=1),
                   ((0, 0), (0, LANES - HEAD_DIM_IDX - N_HEADS_IDX)))
    w_all = jnp.concatenate(
        [w_qa, w_ka, w_va, w_qi, w_kw, w_qr[:, perm], w_kr[:, perm], w_vr, w_gr, w_ga, w_gg], axis=1)
    assert w_all.shape[1] == W_TOTAL
    return w_all.astype(BF16)


def _rotary_tables(pos):
    half = KEY_DIM_R // 2
    inv_freq = ROPE_BASE ** (-jnp.arange(half, dtype=F32) / half)
    ang = pos.astype(F32)[:, None] * inv_freq[None, :]
    cos, sin = jnp.cos(ang), jnp.sin(ang)
    return jnp.tile(cos, (1, 4)), jnp.concatenate([-sin, -sin, sin, sin], axis=1)


def _group(x, pos0, past_k, past_v, past_ki, state, w_all, g_pad, b_pad, t5_bias, log_gamma, fin_w,
           *, tm_proj, qb, sl, ret_chunk, tm_fin):
    bsz, t, _ = x.shape
    m = bsz * t
    x2d = x.reshape(m, D_MODEL)
    pos = pos0 + jnp.arange(t, dtype=jnp.int32)
    cos_t, sin_t = _rotary_tables(pos)
    n_pos_blocks = max(t // tm_proj, 1)
    if tm_proj > t:
        cos_t = jnp.tile(cos_t, (tm_proj // t, 1))
        sin_t = jnp.tile(sin_t, (tm_proj // t, 1))
    emit_vt = past_k is None and tm_proj == sl and t % sl == 0
    (qa, kf, kb, vf, vb, qi, kif, kib, wi, qr, kr, vr, gr, ga, gg) = _proj(
        x2d, w_all, cos_t, sin_t, g_pad, b_pad, tm=tm_proj, n_pos_blocks=n_pos_blocks, emit_vt=emit_vt)

    def b3(a):
        return a.reshape(bsz, t, a.shape[-1])

    n_keys = t if past_k is None else past_k.shape[1] + t
    lp = -(-n_keys // sl) * sl
    kb3, kib3 = b3(kb), b3(kib)
    if emit_vt:
        vt = vb.reshape(bsz, lp // sl, N_HEADS_A, VT_ROWS, sl)
    else:
        vb3 = b3(vb)
        if past_k is not None:
            pk = past_k.reshape(bsz, -1, W_A).astype(BF16)
            pv = past_v.reshape(bsz, -1, W_A).astype(BF16)
            pki = jnp.pad(past_ki, ((0, 0), (0, 0), (0, LANES - HEAD_DIM_IDX))).astype(BF16)
            kb3 = jnp.concatenate([pk, kb3], axis=1)
            vb3 = jnp.concatenate([pv, vb3], axis=1)
            kib3 = jnp.concatenate([pki, kib3], axis=1)
        if lp != n_keys:
            padw = ((0, 0), (0, lp - n_keys), (0, 0))
            kb3, vb3, kib3 = jnp.pad(kb3, padw), jnp.pad(vb3, padw), jnp.pad(kib3, padw)
        vt = vb3.reshape(bsz, lp // sl, sl, W_A).transpose(0, 1, 3, 2)
        vt = vt.reshape(bsz, lp // sl, N_HEADS_A, HEAD_DIM_A, sl)
        vt = jnp.pad(vt, ((0, 0), (0, 0), (0, 0), (0, VT_ROWS - HEAD_DIM_A), (0, 0)), constant_values=1)
    qa3, qi3, wi3 = b3(qa), b3(qi), b3(wi)
    tq = -(-t // qb) * qb
    if tq != t:
        padq = ((0, 0), (0, tq - t), (0, 0))
        qa3, qi3, wi3 = jnp.pad(qa3, padq), jnp.pad(qi3, padq), jnp.pad(wi3, padq)
    a = _attention(qa3, qi3, wi3, kb3, vt, kib3, t5_bias, qb=qb, sl=sl, n_keys=n_keys, pos0=pos0)[:, :t]

    half = KEY_DIM_R // 2
    npair = N_HEADS_R // 2
    if state is None:
        s_init = jnp.zeros((bsz, N_HEADS_R, LANES, VAL_DIM_R), F32)
    else:
        st = state.astype(F32).reshape(bsz, npair, 2, 2, half, VAL_DIM_R)
        zero = jnp.zeros_like(st[:, :, 0])
        s_init = jnp.stack([jnp.stack([st[:, :, 0], zero], axis=3),
                            jnp.stack([zero, st[:, :, 1]], axis=3)], axis=2)
        s_init = s_init.reshape(bsz, N_HEADS_R, LANES, VAL_DIM_R)
    r, s_fin = _retention(b3(qr), b3(kr), b3(vr), b3(gr), fin_w["gn"], s_init, log_gamma, chunk=ret_chunk)
    s6 = s_fin.reshape(bsz, npair, 2, 2, 2, half, VAL_DIM_R)
    s_out = jnp.stack([s6[:, :, 0, :, 0], s6[:, :, 1, :, 1]], axis=2)
    s_out = s_out.reshape(bsz, N_HEADS_R, KEY_DIM_R, VAL_DIM_R)

    y = _finish(x2d, a.reshape(m, W_A), r.reshape(m, W_RV), ga, gg,
                fin_w["wpa"], fin_w["wpr"], fin_w["wo"], fin_w["ln1g"], fin_w["ln1b"],
                fin_w["wg"], fin_w["wu"], fin_w["wd"], fin_w["ln2g"], fin_w["ln2b"], tm=tm_fin)
    return (y.reshape(bsz, t, D_MODEL),
            kf.reshape(bsz, t, N_HEADS_A, HEAD_DIM_A), vf.reshape(bsz, t, N_HEADS_A, HEAD_DIM_A),
            kif.reshape(bsz, t, HEAD_DIM_IDX), s_out)


def kernel(x_prompt, x_sample, cache_k, cache_v, cache_idx_k, state_ret, w_in, idx_k_norm_g, idx_k_norm_b,
           t5_bias, ret_gn_g, w_pa, w_pr, w_o, ln1_g, ln1_b, w_gate, w_up, w_down, ln2_g, ln2_b):
    assert w_in.shape[0] == DEPTH
    log_gamma = jnp.log1p(-jnp.exp2(-5.0 - jnp.arange(N_HEADS_R, dtype=F32)))
    l = 0
    w_all = _relayout_w_in(w_in[l])
    lane_pad = (0, LANES - HEAD_DIM_IDX)
    g_pad = jnp.pad(idx_k_norm_g[l].astype(F32), lane_pad)[None, :]
    b_pad = jnp.pad(idx_k_norm_b[l].astype(F32), lane_pad)[None, :]
    fin_w = dict(
        gn=ret_gn_g[l][None, :].astype(F32),
        wpa=w_pa[l].astype(BF16), wpr=w_pr[l].astype(BF16), wo=w_o[l].astype(BF16),
        ln1g=ln1_g[l][None, :].astype(F32), ln1b=ln1_b[l][None, :].astype(F32),
        wg=w_gate[l].astype(BF16), wu=w_up[l].astype(BF16), wd=w_down[l].astype(BF16),
        ln2g=ln2_g[l][None, :].astype(F32), ln2b=ln2_b[l][None, :].astype(F32))

    t_p = x_prompt.shape[1]
    yp, kp, vp, ikp, sp = _group(
        x_prompt, 0, None, None, None, None, w_all, g_pad, b_pad, t5_bias, log_gamma, fin_w,
        tm_proj=min(512, t_p), qb=min(256, t_p), sl=512, ret_chunk=min(256, t_p), tm_fin=min(512, t_p))

    bs, t_s, _ = x_sample.shape
    past = cache_k.shape[2]
    ys, ks, vs, iks, ss = _group(
        x_sample, past, cache_k[l], cache_v[l], cache_idx_k[l], state_ret[l], w_all, g_pad, b_pad, t5_bias,
        log_gamma, fin_w, tm_proj=bs * t_s, qb=LANES, sl=512, ret_chunk=t_s, tm_fin=bs * t_s)

    return (yp, ys, kp[None], vp[None], ikp[None], sp[None],
            ks[None], vs[None], iks[None], ss.astype(state_ret.dtype)[None])
```

```python
import functools
import math

import numpy as np
import jax
import jax.numpy as jnp
from jax import lax
from jax.experimental import pallas as pl
from jax.experimental.pallas import tpu as pltpu

D_MODEL = 1024
CHUNK = 64
N_HEADS_A = 8
HEAD_DIM_A = 64
N_HEADS_IDX = 4
HEAD_DIM_IDX = 64
TOPK_MAX = 256
NUM_BUCKETS = 32
MAX_DISTANCE = 1024
N_HEADS_R = 8
KEY_DIM_R = 64
VAL_DIM_R = 128
ROPE_BASE = 10000.0
D_FF = 2816
DEPTH = 1
ALPHA = (2.0 * DEPTH) ** 0.25
LN_EPS = 1e-5
GN_EPS = 1e-6

W_A = N_HEADS_A * HEAD_DIM_A
W_IQ = N_HEADS_IDX * HEAD_DIM_IDX
W_RQK = N_HEADS_R * KEY_DIM_R
W_RV = N_HEADS_R * VAL_DIM_R
SPLIT_SIZES = (W_A, W_A, W_A, W_IQ, HEAD_DIM_IDX, N_HEADS_IDX, W_RQK, W_RQK, W_RV, W_RV, D_MODEL, D_MODEL)

LANES = 128
VMEM_LIMIT_BYTES = 60 * 1024 * 1024

OFF_QA = 0
OFF_KA = OFF_QA + W_A
OFF_VA = OFF_KA + W_A
OFF_QI = OFF_VA + W_A
OFF_KW = OFF_QI + W_IQ
OFF_QR = OFF_KW + LANES
OFF_KR = OFF_QR + W_RQK
OFF_VR = OFF_KR + W_RQK
OFF_GR = OFF_VR + W_RV
OFF_GA = OFF_GR + W_RV
OFF_GG = OFF_GA + D_MODEL
W_TOTAL = OFF_GG + D_MODEL

INT_MIN = -(2 ** 31)
NEG_BIG = -1e30
LOG2_E = math.log2(math.e)
FOLD_CHAINS = 8
COUNT_CHAINS = 2
VT_ROWS = HEAD_DIM_A + 16
BF16 = jnp.bfloat16
F32 = jnp.float32
I16 = jnp.int16
ONE16, ZERO16 = np.int16(1), np.int16(0)
MAX16, MIN16 = np.int16(2 ** 15 - 1), np.int16(-(2 ** 15))


def _dot(a, b):
    return jnp.dot(a, b, preferred_element_type=F32)


def _dot_nt(a, b):
    return lax.dot_general(a, b, (((1,), (1,)), ((), ())), preferred_element_type=F32)


def _dot_tn(a, b):
    return lax.dot_general(a, b, (((0,), (0,)), ((), ())), preferred_element_type=F32)


def _layer_norm(z, g, b):
    mu = jnp.mean(z, axis=-1, keepdims=True)
    d = z - mu
    var = jnp.mean(d * d, axis=-1, keepdims=True)
    return d * lax.rsqrt(var + LN_EPS) * g + b


def _const_spec(shape):
    nd = len(shape)
    return pl.BlockSpec(shape, lambda *_: (0,) * nd, pipeline_mode=pl.Buffered(1))


def _proj_kernel(x_ref, w_ref, cos_ref, sin_ref, g_ref, b_ref,
                 qa_o, kf_o, kb_o, vf_o, vb_o, qi_o, kif_o, kib_o, wi_o,
                 qr_o, kr_o, vr_o, gr_o, ga_o, gg_o, *, emit_vt):
    xb = x_ref[...].astype(BF16)

    def seg(off, n):
        return _dot(xb, w_ref[:, off:off + n])

    qa_o[...] = (seg(OFF_QA, W_A) * (HEAD_DIM_A ** -0.5 * LOG2_E)).astype(BF16)
    k = seg(OFF_KA, W_A)
    kf_o[...] = k.reshape(kf_o.shape)
    kb_o[...] = k.astype(BF16)
    v = seg(OFF_VA, W_A)
    vf_o[...] = v.reshape(vf_o.shape)
    if emit_vt:
        v_t = v.T
        for h in range(N_HEADS_A):
            vb_o[h, 0:HEAD_DIM_A, :] = v_t[h * HEAD_DIM_A:(h + 1) * HEAD_DIM_A].astype(BF16)
            vb_o[h, HEAD_DIM_A:VT_ROWS, :] = jnp.ones((VT_ROWS - HEAD_DIM_A, v_t.shape[1]), BF16)
    else:
        vb_o[...] = v.astype(BF16)
    qi_o[...] = seg(OFF_QI, W_IQ).astype(BF16)

    kw = seg(OFF_KW, LANES)
    first = lax.broadcasted_iota(jnp.int32, kw.shape, 1) < HEAD_DIM_IDX
    inv_n = 1.0 / HEAD_DIM_IDX
    mu = jnp.sum(jnp.where(first, kw, 0.0), axis=-1, keepdims=True) * inv_n
    d = kw - mu
    var = jnp.sum(jnp.where(first, d * d, 0.0), axis=-1, keepdims=True) * inv_n
    kin = d * lax.rsqrt(var + LN_EPS) * g_ref[...] + b_ref[...]
    kif_o[...] = kin[:, :HEAD_DIM_IDX]
    kib_o[...] = kin.astype(BF16)

    wi_o[...] = kw * ((N_HEADS_IDX ** -0.5) * (HEAD_DIM_IDX ** -0.5))

    for off, o_ref, scale in ((OFF_QR, qr_o, 1.0), (OFF_KR, kr_o, KEY_DIM_R ** -0.5)):
        h = seg(off, W_RQK)
        for j in range(W_RQK // LANES):
            xj = h[:, j * LANES:(j + 1) * LANES]
            rj = pltpu.roll(xj, LANES // 2, 1)
            oj = xj * cos_ref[...] + rj * sin_ref[...]
            o_ref[:, j * LANES:(j + 1) * LANES] = (oj * scale).astype(BF16)

    vr_o[...] = seg(OFF_VR, W_RV).astype(BF16)
    gr_o[...] = seg(OFF_GR, W_RV).astype(BF16)
    ga_o[...] = seg(OFF_GA, D_MODEL).astype(BF16)
    gg_o[...] = seg(OFF_GG, D_MODEL).astype(BF16)


def _proj(x2d, w_all, cos_t, sin_t, g_pad, b_pad, *, tm, n_pos_blocks, emit_vt):
    m = x2d.shape[0]
    grid = (m // tm,)

    def row(width, dtype):
        return jax.ShapeDtypeStruct((m, width), dtype)

    def rspec(width):
        return pl.BlockSpec((tm, width), lambda i: (i, 0))

    pos_spec = pl.BlockSpec((tm, LANES), lambda i: (i % n_pos_blocks, 0))
    out_shapes = [
        row(W_A, BF16), row(W_A, F32), row(W_A, BF16), row(W_A, F32), row(W_A, BF16),
        row(W_IQ, BF16), row(HEAD_DIM_IDX, F32), row(LANES, BF16), row(LANES, F32),
        row(W_RQK, BF16), row(W_RQK, BF16), row(W_RV, BF16), row(W_RV, BF16),
        row(D_MODEL, BF16), row(D_MODEL, BF16)]
    out_specs = [rspec(s.shape[1]) for s in out_shapes]
    for idx in (1, 3):
        out_shapes[idx] = jax.ShapeDtypeStruct((m, N_HEADS_A, HEAD_DIM_A), F32)
        out_specs[idx] = pl.BlockSpec((tm, N_HEADS_A, HEAD_DIM_A), lambda i: (i, 0, 0))
    if emit_vt:
        out_shapes[4] = jax.ShapeDtypeStruct((m // tm, N_HEADS_A, VT_ROWS, tm), BF16)
        out_specs[4] = pl.BlockSpec((None, N_HEADS_A, VT_ROWS, tm), lambda i: (i, 0, 0, 0))
    return pl.pallas_call(
        functools.partial(_proj_kernel, emit_vt=emit_vt),
        grid=grid,
        in_specs=[rspec(D_MODEL), _const_spec((D_MODEL, W_TOTAL)), pos_spec, pos_spec,
                  _const_spec((1, LANES)), _const_spec((1, LANES))],
        out_specs=tuple(out_specs),
        out_shape=tuple(out_shapes),
        compiler_params=pltpu.CompilerParams(
            dimension_semantics=("arbitrary",), vmem_limit_bytes=VMEM_LIMIT_BYTES),
        name="proj",
    )(x2d, w_all, cos_t, sin_t, g_pad, b_pad)


def _attn_kernel(qa_ref, qi_ref, wi_ref, kb_ref, vt_ref, ki_ref, bias_ref, tri_ref, o_ref,
                 key_scr, half_scr, qt_scr, m_scr, acc_scr, lga_scr, cma_scr, lgb_scr, cmb_scr, tie_scr,
                 *, qb, sl, n_keys, pos0, near_off, ktop):
    i = pl.program_id(1)
    t0 = pos0 + i * qb
    lblk = jnp.minimum(n_keys, ((t0 + qb - 1) // CHUNK + 1) * CHUNK)
    nslab = (lblk + sl - 1) // sl
    n_lane_tiles = qb // LANES

    qi_t = qi_ref[...].astype(F32).T
    qi_h = [qi_t[h * HEAD_DIM_IDX:(h + 1) * HEAD_DIM_IDX].astype(BF16) for h in range(N_HEADS_IDX)]
    w_t = wi_ref[...].T
    w_h = [w_t[HEAD_DIM_IDX + h:HEAD_DIM_IDX + h + 1] for h in range(N_HEADS_IDX)]
    qpos = t0 + lax.broadcasted_iota(jnp.int32, (1, qb), 1)
    qlim = jnp.minimum((lax.shift_right_logical(qpos, int(math.log2(CHUNK))) + 1) * CHUNK, n_keys)
    key_row = lax.broadcasted_iota(jnp.int32, (sl, qb), 0)

    def a_body(j, carry, masked):
        s0 = pl.multiple_of(j * sl, sl)
        kis = ki_ref[pl.ds(s0, sl), 0:HEAD_DIM_IDX]
        s = w_h[0] * jnp.maximum(_dot(kis, qi_h[0]), 0.0)
        for h in range(1, N_HEADS_IDX):
            s = s + w_h[h] * jnp.maximum(_dot(kis, qi_h[h]), 0.0)
        b = lax.bitcast_convert_type(s, jnp.int32)
        skey = jnp.where(b < 0, INT_MIN - b, b)
        if masked:
            skey = jnp.where(s0 + key_row < qlim, skey, INT_MIN)
        key_scr[j] = skey
        half_scr[j] = lax.shift_right_arithmetic(skey, 16).astype(I16)
        return carry

    n_open = jnp.minimum(jnp.minimum((t0 // CHUNK + 1) * CHUNK, n_keys) // sl, nslab)

    def a_pair(jj, carry):
        a_body(2 * jj, carry, False)
        return a_body(2 * jj + 1, carry, False)

    lax.fori_loop(0, n_open // 2, a_pair, 0)
    lax.fori_loop(2 * (n_open // 2), nslab, functools.partial(a_body, masked=True), 0)
    half_scr[nslab] = jnp.full((sl, qb), MIN16, I16)

    def count_ge16(cand16):
        def c_body(jj, acc):
            for j in (2 * jj, 2 * jj + 1):
                hit = jnp.where(half_scr[j] >= cand16, ONE16, ZERO16)
                hit = hit.reshape(sl // (16 * COUNT_CHAINS), COUNT_CHAINS, 16, qb)
                for a in range(hit.shape[0]):
                    acc = acc + hit[a]
            return acc
        acc = lax.fori_loop(0, (nslab + 1) // 2, c_body, jnp.zeros((COUNT_CHAINS, 16, qb), I16))
        return jnp.sum(acc.astype(F32).reshape(COUNT_CHAINS * 16, qb), axis=0, keepdims=True)

    def radix16(n_above):
        def bit_body(it, carry):
            t_u, n_above = carry
            cand_u = t_u | jnp.left_shift(jnp.int32(1), 15 - it)
            cnt = count_ge16((cand_u - 2 ** 15).astype(I16))
            take = cnt >= ktop
            return jnp.where(take, cand_u, t_u), jnp.where(take, n_above, cnt)
        return lax.fori_loop(0, 16, bit_body, (jnp.zeros((1, qb), jnp.int32), n_above))

    hi_u, n_above_hi = radix16(jnp.zeros((1, qb), F32))
    hi16 = (hi_u - 2 ** 15).astype(I16)

    def low_body(j, carry):
        low = ((key_scr[j] & 0xFFFF) - 2 ** 15).astype(I16)
        high = half_scr[j]
        half_scr[j] = jnp.where(high == hi16, low, jnp.where(high > hi16, MAX16, MIN16))
        return carry

    lax.fori_loop(0, nslab, low_body, 0)
    low_u, n_gt = radix16(n_above_hi)
    thr = (hi_u - 2 ** 15) * 2 ** 16 + low_u
    need = jnp.where(thr == INT_MIN, 0.0, ktop - n_gt)

    tie_scr[...] = jnp.zeros(tie_scr.shape, F32)

    def mask_slab(j):
        keys = key_scr[j]
        eq = keys == thr
        pref = _dot(tri_ref[...], jnp.where(eq, 1.0, 0.0).astype(BF16))
        seen = tie_scr[...]
        sel = (keys > thr) | (eq & (pref + seen <= need))
        key_scr[j] = lax.bitcast_convert_type(jnp.where(sel, 0.0, NEG_BIG), jnp.int32)
        tie_scr[...] = seen + pref[sl - 1:sl]

    jfar = jnp.minimum(jnp.maximum(t0 - near_off + sl - 1, 0) // sl, nslab)
    n_pairs = N_HEADS_A // 2
    qa_t = qa_ref[...].astype(F32).T
    low = lax.broadcasted_iota(jnp.int32, (LANES, qb), 0) < HEAD_DIM_A
    for p in range(n_pairs):
        blk = qa_t[p * LANES:(p + 1) * LANES]
        qt_scr[p, :, 0:qb] = jnp.where(low, blk, 0.0).astype(BF16)
        qt_scr[p, :, qb:2 * qb] = jnp.where(low, 0.0, blk).astype(BF16)
    m_scr[...] = jnp.full(m_scr.shape, NEG_BIG, F32)
    acc_scr[...] = jnp.zeros(acc_scr.shape, F32)

    bufs = ((lga_scr, cma_scr), (lgb_scr, cmb_scr))

    def logits_pair(j, p, buf, near):
        lg_scr, cm_scr = bufs[buf]
        s0 = pl.multiple_of(j * sl, sl)
        madd = lax.bitcast_convert_type(key_scr[j], F32)
        k2 = kb_ref[pl.ds(s0, sl), p * LANES:(p + 1) * LANES]
        l2 = _dot(k2, qt_scr[p])
        for e in range(2):
            h = 2 * p + e
            logit = l2[:, e * qb:(e + 1) * qb] + madd
            if near:
                c0 = near_off - t0 + s0 + qb - LANES + sl
                logit = logit + jnp.concatenate(
                    [bias_ref[h, pl.ds(pl.multiple_of(c0 - g * LANES, LANES), sl), :]
                     for g in range(n_lane_tiles)], axis=1)
            logit = logit.astype(BF16)
            lg_scr[h] = logit
            part = jnp.max(logit.reshape(sl // (16 * FOLD_CHAINS), FOLD_CHAINS, 16, qb), axis=0)
            cm_scr[h] = jnp.max(jnp.max(part, axis=0).astype(F32), axis=0, keepdims=True)

    def softmax_head(j, h, buf):
        lg_scr, cm_scr = bufs[buf]
        m_prev = m_scr[h]
        m_new = jnp.maximum(m_prev, cm_scr[h])
        alpha = jnp.exp2(m_prev - m_new)
        pexp = jnp.exp2(lg_scr[h] - m_new.astype(BF16))
        acc_scr[h] = alpha * acc_scr[h] + _dot(vt_ref[j, h], pexp)
        m_scr[h] = m_new

    def logits_stage(j, buf, near):
        for p in range(n_pairs):
            logits_pair(j, p, buf, near)

    def softmax_stage(j, buf):
        for h in range(N_HEADS_A):
            softmax_head(j, h, buf)

    def overlapped(j_soft, buf_soft, j_logits, near):
        mask_slab(j_logits)
        for p in range(n_pairs):
            logits_pair(j_logits, p, 1 - buf_soft, near)
            softmax_head(j_soft, 2 * p, buf_soft)
            softmax_head(j_soft, 2 * p + 1, buf_soft)

    def d_body(jj, carry, near):
        j = 2 * jj
        overlapped(j, 0, j + 1, near)
        overlapped(j + 1, 1, j + 2, near)
        return carry

    def odd_tail():
        softmax_stage(nslab - 1, 0)

    def even_tail():
        overlapped(nslab - 2, 0, nslab - 1, True)
        softmax_stage(nslab - 1, 1)

    mask_slab(0)
    lax.cond(jfar > 0, lambda: logits_stage(0, 0, False), lambda: logits_stage(0, 0, True))
    n_trips = (nslab - 1) // 2
    far_trips = jnp.minimum(jnp.maximum(jfar - 1, 0) // 2, n_trips)
    lax.fori_loop(0, far_trips, functools.partial(d_body, near=False), 0)
    lax.fori_loop(far_trips, n_trips, functools.partial(d_body, near=True), 0)
    lax.cond(lax.rem(nslab, 2) == 1, odd_tail, even_tail)

    for p in range(n_pairs):
        outs = [acc_scr[h, 0:HEAD_DIM_A] / acc_scr[h, HEAD_DIM_A:HEAD_DIM_A + 1] for h in (2 * p, 2 * p + 1)]
        o_ref[:, p * LANES:(p + 1) * LANES] = jnp.concatenate(outs, axis=0).T.astype(o_ref.dtype)


def _t5_bucket_np(rel):
    nb = NUM_BUCKETS // 2
    max_exact = nb // 2
    ret = np.where(rel > 0, nb, 0)
    n = np.abs(rel)
    nf = np.maximum(n, max_exact).astype(np.float64)
    large = max_exact + (np.log(nf / max_exact) / math.log(MAX_DISTANCE / max_exact) * (nb - max_exact)).astype(np.int64)
    large = np.minimum(large, nb - 1)
    return ret + np.where(n < max_exact, n, large)


def _t5_bucket(rel):
    nb = NUM_BUCKETS // 2
    max_exact = nb // 2
    ret = jnp.where(rel > 0, nb, 0)
    n = jnp.abs(rel)
    nf = jnp.maximum(n, max_exact).astype(F32)
    large = max_exact + (jnp.log(nf / max_exact) / math.log(MAX_DISTANCE / max_exact) * (nb - max_exact)).astype(jnp.int32)
    large = jnp.minimum(large, nb - 1)
    return ret + jnp.where(n < max_exact, n, large)


def _attention(qa, qi, wi, kb, vt, kib, t5_bias, *, qb, sl, n_keys, pos0):
    bsz, tq, _ = qa.shape
    lp = kb.shape[1]
    assert tq % qb == 0 and lp % sl == 0 and sl % LANES == 0 and qb % LANES == 0
    ktop = min(TOPK_MAX, n_keys // 4)
    nslab_max = lp // sl

    gran = math.gcd(pos0, sl) if tq == qb else math.gcd(math.gcd(pos0, qb), sl)
    assert gran % LANES == 0
    rel_all = np.arange(-(n_keys + qb), 0)
    far_bucket = _t5_bucket_np(np.array([-(n_keys + qb)]))[0]
    sat = rel_all[_t5_bucket_np(rel_all) != far_bucket]
    n_sat = int(-sat.min()) + 1 if sat.size else 1
    near_off = -(-(sl - 1 - gran + n_sat) // gran) * gran
    rel_min = -(qb - 1) - near_off
    rels = jnp.arange(rel_min, sl, dtype=jnp.int32)
    tab = (t5_bias[_t5_bucket(rels)] - t5_bias[far_bucket][None, :]) * LOG2_E
    n_rel = sl - rel_min
    n_rows = near_off + sl + qb - LANES
    assert n_rows + LANES - 1 == n_rel
    rolled = jnp.roll(tab.T, -(LANES - 1), axis=1)
    toep = jnp.tile(rolled, (1, LANES))[:, :LANES * (n_rel - 1)].reshape(N_HEADS_A, LANES, n_rel - 1)
    master = jnp.transpose(toep[:, :, :n_rows], (0, 2, 1))
    master = jnp.pad(master, ((0, 0), (sl, 0), (0, 0)))

    tri = jnp.asarray(np.tril(np.ones((sl, sl), np.float32)), dtype=BF16)

    kern = functools.partial(_attn_kernel, qb=qb, sl=sl, n_keys=n_keys, pos0=pos0,
                             near_off=near_off, ktop=float(ktop))

    def qspec(width_):
        return pl.BlockSpec((None, qb, width_), lambda b, i: (b, i, 0))

    def kspec(*shape):
        nd = len(shape)
        return pl.BlockSpec((None,) + shape, lambda b, i: (b,) + (0,) * nd, pipeline_mode=pl.Buffered(1))

    return pl.pallas_call(
        kern,
        grid=(bsz, tq // qb),
        in_specs=[qspec(W_A), qspec(W_IQ), qspec(LANES), kspec(lp, W_A),
                  kspec(nslab_max, N_HEADS_A, VT_ROWS, sl),
                  kspec(lp, LANES), _const_spec(master.shape), _const_spec((sl, sl))],
        out_specs=qspec(W_A),
        out_shape=jax.ShapeDtypeStruct((bsz, tq, W_A), BF16),
        scratch_shapes=[pltpu.VMEM((nslab_max, sl, qb), jnp.int32),
                        pltpu.VMEM((nslab_max + 1, sl, qb), I16),
                        pltpu.VMEM((N_HEADS_A // 2, LANES, 2 * qb), BF16),
                        pltpu.VMEM((N_HEADS_A, 1, qb), F32),
                        pltpu.VMEM((N_HEADS_A, VT_ROWS, qb), F32),
                        pltpu.VMEM((N_HEADS_A, sl, qb), BF16), pltpu.VMEM((N_HEADS_A, 1, qb), F32),
                        pltpu.VMEM((N_HEADS_A, sl, qb), BF16), pltpu.VMEM((N_HEADS_A, 1, qb), F32),
                        pltpu.VMEM((1, qb), F32)],
        compiler_params=pltpu.CompilerParams(
            dimension_semantics=("arbitrary", "arbitrary"), vmem_limit_bytes=VMEM_LIMIT_BYTES),
        name="dsa_attention",
    )(qa, qi, wi, kb, vt, kib, master, tri)


def _ret_kernel(q_ref, k_ref, v_ref, g_ref, gn_ref, s0_ref, dm_ref, qd_ref, kd_ref, gc_ref,
                r_ref, sfin_ref, s_scr, *, n_chunks):
    c = pl.program_id(1)

    @pl.when(c == 0)
    def _():
        s_scr[...] = s0_ref[...]

    for h in range(N_HEADS_R):
        p = h // 2
        q2 = q_ref[:, p * LANES:(p + 1) * LANES].astype(F32)
        k2 = k_ref[:, p * LANES:(p + 1) * LANES]
        vh = v_ref[:, h * VAL_DIM_R:(h + 1) * VAL_DIM_R]
        s_h = s_scr[h]
        qm = (q2 * qd_ref[h, 0]).astype(BF16)
        qdec = (q2 * qd_ref[h, 1]).astype(BF16)
        kdec = (k2.astype(F32) * kd_ref[h]).astype(BF16)
        inner = _dot_nt(qm, k2) * dm_ref[h]
        o = _dot(inner.astype(BF16), vh) + _dot(qdec, s_h.astype(BF16))
        s_scr[h] = gc_ref[h] * s_h + _dot_tn(kdec, vh)
        mu = jnp.mean(o, axis=-1, keepdims=True)
        d = o - mu
        var = jnp.mean(d * d, axis=-1, keepdims=True)
        on = d * lax.rsqrt(var + GN_EPS) * gn_ref[:, h * VAL_DIM_R:(h + 1) * VAL_DIM_R]
        gate = g_ref[:, h * VAL_DIM_R:(h + 1) * VAL_DIM_R].astype(F32)
        r_ref[:, h * VAL_DIM_R:(h + 1) * VAL_DIM_R] = (on * (gate * jax.nn.sigmoid(gate))).astype(r_ref.dtype)

    @pl.when(c == n_chunks - 1)
    def _():
        sfin_ref[...] = s_scr[...]


def _pair_lane_heads():
    return (np.arange(LANES) // (KEY_DIM_R // 2)) % 2


def _retention(qr, kr, vr, gr, gn_g, s_init, log_gamma, *, chunk):
    bsz, t, _ = qr.shape
    n_chunks = t // chunk
    n = jnp.arange(chunk, dtype=F32)
    lg = log_gamma.astype(F32)
    diff = n[:, None] - n[None, :]
    dmask = jnp.where(diff >= 0, jnp.exp(lg[:, None, None] * jnp.maximum(diff, 0.0)), 0.0)
    owner = jnp.asarray(_pair_lane_heads()[None, :] == (np.arange(N_HEADS_R) % 2)[:, None], F32)
    q_dec = jnp.exp(lg[:, None] * (n + 1.0))
    k_dec = jnp.exp(lg[:, None] * (chunk - 1.0 - n))
    qd = jnp.stack([jnp.broadcast_to(owner[:, None, :], (N_HEADS_R, chunk, LANES)),
                    owner[:, None, :] * q_dec[:, :, None]], axis=1)
    kd = owner[:, None, :] * k_dec[:, :, None]
    gc = jnp.broadcast_to(jnp.exp(lg * chunk)[:, None, None], (N_HEADS_R, 1, LANES))

    def tspec(width):
        return pl.BlockSpec((None, chunk, width), lambda b, c: (b, c, 0))

    sspec = pl.BlockSpec((None, N_HEADS_R, LANES, VAL_DIM_R), lambda b, c: (b, 0, 0, 0))
    return pl.pallas_call(
        functools.partial(_ret_kernel, n_chunks=n_chunks),
        grid=(bsz, n_chunks),
        in_specs=[tspec(W_RQK), tspec(W_RQK), tspec(W_RV), tspec(W_RV), _const_spec((1, W_RV)), sspec,
                  _const_spec(dmask.shape), _const_spec(qd.shape), _const_spec(kd.shape),
                  _const_spec(gc.shape)],
        out_specs=(tspec(W_RV), sspec),
        out_shape=(jax.ShapeDtypeStruct((bsz, t, W_RV), BF16),
                   jax.ShapeDtypeStruct((bsz, N_HEADS_R, LANES, VAL_DIM_R), F32)),
        scratch_shapes=[pltpu.VMEM((N_HEADS_R, LANES, VAL_DIM_R), F32)],
        compiler_params=pltpu.CompilerParams(
            dimension_semantics=("arbitrary", "arbitrary"), vmem_limit_bytes=VMEM_LIMIT_BYTES),
        name="retention",
    )(qr, kr, vr, gr, gn_g, s_init, dmask, qd, kd, gc)


def _finish_kernel(x_ref, a_ref, r_ref, ga_ref, gg_ref, wpa, wpr, wo, ln1g, ln1b, wg, wu, wd, ln2g, ln2b,
                   y_ref):
    x = x_ref[...]
    merged = (jax.nn.sigmoid(ga_ref[...].astype(F32)) * _dot(a_ref[...], wpa[...])
              + jax.nn.sigmoid(gg_ref[...].astype(F32)) * _dot(r_ref[...], wpr[...]))
    x1 = _layer_norm(ALPHA * x + _dot(merged.astype(BF16), wo[...]), ln1g[...], ln1b[...])
    x1b = x1.astype(BF16)
    gate = _dot(x1b, wg[...])
    hidden = gate * jax.nn.sigmoid(gate) * _dot(x1b, wu[...])
    y = _layer_norm(ALPHA * x1 + _dot(hidden.astype(BF16), wd[...]), ln2g[...], ln2b[...])
    y_ref[...] = y


def _finish(x2d, a, r, ga, gg, wpa, wpr, wo, ln1g, ln1b, wg, wu, wd, ln2g, ln2b, *, tm):
    m = x2d.shape[0]

    def rspec(width):
        return pl.BlockSpec((tm, width), lambda i: (i, 0))

    consts = (wpa, wpr, wo, ln1g, ln1b, wg, wu, wd, ln2g, ln2b)
    return pl.pallas_call(
        _finish_kernel,
        grid=(m // tm,),
        in_specs=[rspec(D_MODEL), rspec(W_A), rspec(W_RV), rspec(D_MODEL), rspec(D_MODEL)]
        + [_const_spec(c.shape) for c in consts],
        out_specs=rspec(D_MODEL),
        out_shape=jax.ShapeDtypeStruct((m, D_MODEL), F32),
        compiler_params=pltpu.CompilerParams(
            dimension_semantics=("arbitrary",), vmem_limit_bytes=VMEM_LIMIT_BYTES),
        name="finish",
    )(x2d, a, r, ga, gg, *consts)


def _relayout_w_in(w_in):
    offs = np.cumsum((0,) + SPLIT_SIZES)
    parts = [w_in[:, offs[i]:offs[i + 1]] for i in range(len(SPLIT_SIZES))]
    w_qa, w_ka, w_va, w_qi, w_ki, w_wi, w_qr, w_kr, w_vr, w_gr, w_ga, w_gg = parts
    half = KEY_DIM_R // 2
    perm = np.concatenate([
        np.arange(hh * KEY_DIM_R + part * half, hh * KEY_DIM_R + (part + 1) * half)
        for p in range(N_HEADS_R // 2) for part in (0, 1) for hh in (2 * p, 2 * p + 1)])
    w_kw = jnp.pad(jnp.concatenate([w_ki, w_wi], axis=1),
                   ((0, 0), (0, LANES - HEAD_DIM_IDX - N_HEADS_IDX)))
    w_all = jnp.concatenate(
        [w_qa, w_ka, w_va, w_qi, w_kw, w_qr[:, perm], w_kr[:, perm], w_vr, w_gr, w_ga, w_gg], axis=1)
    assert w_all.shape[1] == W_TOTAL
    return w_all.astype(BF16)


def _rotary_tables(pos):
    half = KEY_DIM_R // 2
    inv_freq = ROPE_BASE ** (-jnp.arange(half, dtype=F32) / half)
    ang = pos.astype(F32)[:, None] * inv_freq[None, :]
    cos, sin = jnp.cos(ang), jnp.sin(ang)
    return jnp.tile(cos, (1, 4)), jnp.concatenate([-sin, -sin, sin, sin], axis=1)


def _group(x, pos0, past_k, past_v, past_ki, state, w_all, g_pad, b_pad, t5_bias, log_gamma, fin_w,
           *, tm_proj, qb, sl, ret_chunk, tm_fin):
    bsz, t, _ = x.shape
    m = bsz * t
    x2d = x.reshape(m, D_MODEL)
    pos = pos0 + jnp.arange(t, dtype=jnp.int32)
    cos_t, sin_t = _rotary_tables(pos)
    n_pos_blocks = max(t // tm_proj, 1)
    if tm_proj > t:
        cos_t = jnp.tile(cos_t, (tm_proj // t, 1))
        sin_t = jnp.tile(sin_t, (tm_proj // t, 1))
    emit_vt = past_k is None and tm_proj == sl and t % sl == 0
    (qa, kf, kb, vf, vb, qi, kif, kib, wi, qr, kr, vr, gr, ga, gg) = _proj(
        x2d, w_all, cos_t, sin_t, g_pad, b_pad, tm=tm_proj, n_pos_blocks=n_pos_blocks, emit_vt=emit_vt)

    def b3(a):
        return a.reshape(bsz, t, a.shape[-1])

    n_keys = t if past_k is None else past_k.shape[1] + t
    lp = -(-n_keys // sl) * sl
    kb3, kib3 = b3(kb), b3(kib)
    if emit_vt:
        vt = vb.reshape(bsz, lp // sl, N_HEADS_A, VT_ROWS, sl)
    else:
        vb3 = b3(vb)
        if past_k is not None:
            pk = past_k.reshape(bsz, -1, W_A).astype(BF16)
            pv = past_v.reshape(bsz, -1, W_A).astype(BF16)
            pki = jnp.pad(past_ki, ((0, 0), (0, 0), (0, LANES - HEAD_DIM_IDX))).astype(BF16)
            kb3 = jnp.concatenate([pk, kb3], axis=1)
            vb3 = jnp.concatenate([pv, vb3], axis=1)
            kib3 = jnp.concatenate([pki, kib3], axis=1)
        if lp != n_keys:
            padw = ((0, 0), (0, lp - n_keys), (0, 0))
            kb3, vb3, kib3 = jnp.pad(kb3, padw), jnp.pad(vb3, padw), jnp.pad(kib3, padw)
        vt = vb3.reshape(bsz, lp // sl, sl, W_A).transpose(0, 1, 3, 2)
        vt = vt.reshape(bsz, lp // sl, N_HEADS_A, HEAD_DIM_A, sl)
        vt = jnp.pad(vt, ((0, 0), (0, 0), (0, 0), (0, VT_ROWS - HEAD_DIM_A), (0, 0)), constant_values=1)
    qa3, qi3, wi3 = b3(qa), b3(qi), b3(wi)
    tq = -(-t // qb) * qb
    if tq != t:
        padq = ((0, 0), (0, tq - t), (0, 0))
        qa3, qi3, wi3 = jnp.pad(qa3, padq), jnp.pad(qi3, padq), jnp.pad(wi3, padq)
    a = _attention(qa3, qi3, wi3, kb3, vt, kib3, t5_bias, qb=qb, sl=sl, n_keys=n_keys, pos0=pos0)[:, :t]

    half = KEY_DIM_R // 2
    npair = N_HEADS_R // 2
    if state is None:
        s_init = jnp.zeros((bsz, N_HEADS_R, LANES, VAL_DIM_R), F32)
    else:
        st = state.astype(F32).reshape(bsz, npair, 2, 2, half, VAL_DIM_R)
        zero = jnp.zeros_like(st[:, :, 0])
        s_init = jnp.stack([jnp.stack([st[:, :, 0], zero], axis=3),
                            jnp.stack([zero, st[:, :, 1]], axis=3)], axis=2)
        s_init = s_init.reshape(bsz, N_HEADS_R, LANES, VAL_DIM_R)
    r, s_fin = _retention(b3(qr), b3(kr), b3(vr), b3(gr), fin_w["gn"], s_init, log_gamma, chunk=ret_chunk)
    s6 = s_fin.reshape(bsz, npair, 2, 2, 2, half, VAL_DIM_R)
    s_out = jnp.stack([s6[:, :, 0, :, 0], s6[:, :, 1, :, 1]], axis=2)
    s_out = s_out.reshape(bsz, N_HEADS_R, KEY_DIM_R, VAL_DIM_R)

    y = _finish(x2d, a.reshape(m, W_A), r.reshape(m, W_RV), ga, gg,
                fin_w["wpa"], fin_w["wpr"], fin_w["wo"], fin_w["ln1g"], fin_w["ln1b"],
                fin_w["wg"], fin_w["wu"], fin_w["wd"], fin_w["ln2g"], fin_w["ln2b"], tm=tm_fin)
    return (y.reshape(bsz, t, D_MODEL),
            kf.reshape(bsz, t, N_HEADS_A, HEAD_DIM_A), vf.reshape(bsz, t, N_HEADS_A, HEAD_DIM_A),
            kif.reshape(bsz, t, HEAD_DIM_IDX), s_out)


def kernel(x_prompt, x_sample, cache_k, cache_v, cache_idx_k, state_ret, w_in, idx_k_norm_g, idx_k_norm_b,
           t5_bias, ret_gn_g, w_pa, w_pr, w_o, ln1_g, ln1_b, w_gate, w_up, w_down, ln2_g, ln2_b):
    assert w_in.shape[0] == DEPTH
    log_gamma = jnp.log1p(-jnp.exp2(-5.0 - jnp.arange(N_HEADS_R, dtype=F32)))
    l = 0
    w_all = _relayout_w_in(w_in[l])
    lane_pad = (0, LANES - HEAD_DIM_IDX)
    g_pad = jnp.pad(idx_k_norm_g[l].astype(F32), lane_pad)[None, :]
    b_pad = jnp.pad(idx_k_norm_b[l].astype(F32), lane_pad)[None, :]
    fin_w = dict(
        gn=ret_gn_g[l][None, :].astype(F32),
        wpa=w_pa[l].astype(BF16), wpr=w_pr[l].astype(BF16), wo=w_o[l].astype(BF16),
        ln1g=ln1_g[l][None, :].astype(F32), ln1b=ln1_b[l][None, :].astype(F32),
        wg=w_gate[l].astype(BF16), wu=w_up[l].astype(BF16), wd=w_down[l].astype(BF16),
        ln2g=ln2_g[l][None, :].astype(F32), ln2b=ln2_b[l][None, :].astype(F32))

    t_p = x_prompt.shape[1]
    yp, kp, vp, ikp, sp = _group(
        x_prompt, 0, None, None, None, None, w_all, g_pad, b_pad, t5_bias, log_gamma, fin_w,
        tm_proj=min(512, t_p), qb=min(256, t_p), sl=512, ret_chunk=min(256, t_p), tm_fin=min(512, t_p))

    bs, t_s, _ = x_sample.shape
    past = cache_k.shape[2]
    ys, ks, vs, iks, ss = _group(
        x_sample, past, cache_k[l], cache_v[l], cache_idx_k[l], state_ret[l], w_all, g_pad, b_pad, t5_bias,
        log_gamma, fin_w, tm_proj=bs * t_s, qb=LANES, sl=512, ret_chunk=t_s, tm_fin=bs * t_s)

    return (yp, ys, kp[None], vp[None], ikp[None], sp[None],
            ks[None], vs[None], iks[None], ss.astype(state_ret.dtype)[None])
```

```python
import functools
import math

import numpy as np
import jax
import jax.numpy as jnp
from jax import lax
from jax.experimental import pallas as pl
from jax.experimental.pallas import tpu as pltpu

D_MODEL = 1024
CHUNK = 64
N_HEADS_A = 8
HEAD_DIM_A = 64
N_HEADS_IDX = 4
HEAD_DIM_IDX = 64
TOPK_MAX = 256
NUM_BUCKETS = 32
MAX_DISTANCE = 1024
N_HEADS_R = 8
KEY_DIM_R = 64
VAL_DIM_R = 128
ROPE_BASE = 10000.0
D_FF = 2816
DEPTH = 1
ALPHA = (2.0 * DEPTH) ** 0.25
LN_EPS = 1e-5
GN_EPS = 1e-6

W_A = N_HEADS_A * HEAD_DIM_A
W_IQ = N_HEADS_IDX * HEAD_DIM_IDX
W_RQK = N_HEADS_R * KEY_DIM_R
W_RV = N_HEADS_R * VAL_DIM_R
SPLIT_SIZES = (W_A, W_A, W_A, W_IQ, HEAD_DIM_IDX, N_HEADS_IDX, W_RQK, W_RQK, W_RV, W_RV, D_MODEL, D_MODEL)

LANES = 128
VMEM_LIMIT_BYTES = 60 * 1024 * 1024

OFF_QA = 0
OFF_KA = OFF_QA + W_A
OFF_VA = OFF_KA + W_A
OFF_QI = OFF_VA + W_A
OFF_KW = OFF_QI + W_IQ
OFF_QR = OFF_KW + LANES
OFF_KR = OFF_QR + W_RQK
OFF_VR = OFF_KR + W_RQK
OFF_GR = OFF_VR + W_RV
OFF_GA = OFF_GR + W_RV
OFF_GG = OFF_GA + D_MODEL
W_TOTAL = OFF_GG + D_MODEL

INT_MIN = -(2 ** 31)
NEG_BIG = -1e30
LOG2_E = math.log2(math.e)
FOLD_CHAINS = 8
COUNT_CHAINS = 2
VT_ROWS = HEAD_DIM_A + 16
BF16 = jnp.bfloat16
F32 = jnp.float32
I16 = jnp.int16
ONE16, ZERO16 = np.int16(1), np.int16(0)
MAX16, MIN16 = np.int16(2 ** 15 - 1), np.int16(-(2 ** 15))


def _dot(a, b):
    return jnp.dot(a, b, preferred_element_type=F32)


def _dot_nt(a, b):
    return lax.dot_general(a, b, (((1,), (1,)), ((), ())), preferred_element_type=F32)


def _dot_tn(a, b):
    return lax.dot_general(a, b, (((0,), (0,)), ((), ())), preferred_element_type=F32)


def _layer_norm(z, g, b):
    mu = jnp.mean(z, axis=-1, keepdims=True)
    d = z - mu
    var = jnp.mean(d * d, axis=-1, keepdims=True)
    return d * lax.rsqrt(var + LN_EPS) * g + b


def _const_spec(shape):
    nd = len(shape)
    return pl.BlockSpec(shape, lambda *_: (0,) * nd, pipeline_mode=pl.Buffered(1))


def _proj_kernel(x_ref, w_ref, cos_ref, sin_ref, g_ref, b_ref,
                 qa_o, kf_o, kb_o, vf_o, vb_o, qi_o, kif_o, kib_o, wi_o,
                 qr_o, kr_o, vr_o, gr_o, ga_o, gg_o, *, emit_vt):
    xb = x_ref[...].astype(BF16)

    def seg(off, n):
        return _dot(xb, w_ref[:, off:off + n])

    qa_o[...] = (seg(OFF_QA, W_A) * (HEAD_DIM_A ** -0.5 * LOG2_E)).astype(BF16)
    k = seg(OFF_KA, W_A)
    kf_o[...] = k.reshape(kf_o.shape)
    kb_o[...] = k.astype(BF16)
    v = seg(OFF_VA, W_A)
    vf_o[...] = v.reshape(vf_o.shape)
    if emit_vt:
        v_t = v.T
        for h in range(N_HEADS_A):
            vb_o[h, 0:HEAD_DIM_A, :] = v_t[h * HEAD_DIM_A:(h + 1) * HEAD_DIM_A].astype(BF16)
            vb_o[h, HEAD_DIM_A:VT_ROWS, :] = jnp.ones((VT_ROWS - HEAD_DIM_A, v_t.shape[1]), BF16)
    else:
        vb_o[...] = v.astype(BF16)
    qi_o[...] = seg(OFF_QI, W_IQ).astype(BF16)

    kw = seg(OFF_KW, LANES)
    first = lax.broadcasted_iota(jnp.int32, kw.shape, 1) < HEAD_DIM_IDX
    inv_n = 1.0 / HEAD_DIM_IDX
    mu = jnp.sum(jnp.where(first, kw, 0.0), axis=-1, keepdims=True) * inv_n
    d = kw - mu
    var = jnp.sum(jnp.where(first, d * d, 0.0), axis=-1, keepdims=True) * inv_n
    kin = d * lax.rsqrt(var + LN_EPS) * g_ref[...] + b_ref[...]
    kif_o[...] = kin[:, :HEAD_DIM_IDX]
    kib_o[...] = kin.astype(BF16)

    wi_o[...] = kw * ((N_HEADS_IDX ** -0.5) * (HEAD_DIM_IDX ** -0.5))

    for off, o_ref, scale in ((OFF_QR, qr_o, 1.0), (OFF_KR, kr_o, KEY_DIM_R ** -0.5)):
        h = seg(off, W_RQK)
        for j in range(W_RQK // LANES):
            xj = h[:, j * LANES:(j + 1) * LANES]
            rj = pltpu.roll(xj, LANES // 2, 1)
            oj = xj * cos_ref[...] + rj * sin_ref[...]
            o_ref[:, j * LANES:(j + 1) * LANES] = (oj * scale).astype(BF16)

    vr_o[...] = seg(OFF_VR, W_RV).astype(BF16)
    gr_o[...] = seg(OFF_GR, W_RV).astype(BF16)
    ga_o[...] = seg(OFF_GA, D_MODEL).astype(BF16)
    gg_o[...] = seg(OFF_GG, D_MODEL).astype(BF16)


def _proj(x2d, w_all, cos_t, sin_t, g_pad, b_pad, *, tm, n_pos_blocks, emit_vt):
    m = x2d.shape[0]
    grid = (m // tm,)

    def row(width, dtype):
        return jax.ShapeDtypeStruct((m, width), dtype)

    def rspec(width):
        return pl.BlockSpec((tm, width), lambda i: (i, 0))

    pos_spec = pl.BlockSpec((tm, LANES), lambda i: (i % n_pos_blocks, 0))
    out_shapes = [
        row(W_A, BF16), row(W_A, F32), row(W_A, BF16), row(W_A, F32), row(W_A, BF16),
        row(W_IQ, BF16), row(HEAD_DIM_IDX, F32), row(LANES, BF16), row(LANES, F32),
        row(W_RQK, BF16), row(W_RQK, BF16), row(W_RV, BF16), row(W_RV, BF16),
        row(D_MODEL, BF16), row(D_MODEL, BF16)]
    out_specs = [rspec(s.shape[1]) for s in out_shapes]
    for idx in (1, 3):
        out_shapes[idx] = jax.ShapeDtypeStruct((m, N_HEADS_A, HEAD_DIM_A), F32)
        out_specs[idx] = pl.BlockSpec((tm, N_HEADS_A, HEAD_DIM_A), lambda i: (i, 0, 0))
    if emit_vt:
        out_shapes[4] = jax.ShapeDtypeStruct((m // tm, N_HEADS_A, VT_ROWS, tm), BF16)
        out_specs[4] = pl.BlockSpec((None, N_HEADS_A, VT_ROWS, tm), lambda i: (i, 0, 0, 0))
    return pl.pallas_call(
        functools.partial(_proj_kernel, emit_vt=emit_vt),
        grid=grid,
        in_specs=[rspec(D_MODEL), _const_spec((D_MODEL, W_TOTAL)), pos_spec, pos_spec,
                  _const_spec((1, LANES)), _const_spec((1, LANES))],
        out_specs=tuple(out_specs),
        out_shape=tuple(out_shapes),
        compiler_params=pltpu.CompilerParams(
            dimension_semantics=("arbitrary",), vmem_limit_bytes=VMEM_LIMIT_BYTES),
        name="proj",
    )(x2d, w_all, cos_t, sin_t, g_pad, b_pad)


def _attn_kernel(qa_ref, qi_ref, wi_ref, kb_ref, vt_ref, ki_ref, bias_ref, tri_ref, o_ref,
                 key_scr, half_scr, qt_scr, m_scr, acc_scr, lga_scr, cma_scr, lgb_scr, cmb_scr, tie_scr,
                 *, qb, sl, n_keys, pos0, near_off, ktop):
    i = pl.program_id(1)
    t0 = pos0 + i * qb
    lblk = jnp.minimum(n_keys, ((t0 + qb - 1) // CHUNK + 1) * CHUNK)
    nslab = (lblk + sl - 1) // sl
    n_lane_tiles = qb // LANES

    qi_t = qi_ref[...].astype(F32).T
    qi_h = [qi_t[h * HEAD_DIM_IDX:(h + 1) * HEAD_DIM_IDX].astype(BF16) for h in range(N_HEADS_IDX)]
    w_t = wi_ref[...].T
    w_h = [w_t[HEAD_DIM_IDX + h:HEAD_DIM_IDX + h + 1] for h in range(N_HEADS_IDX)]
    qpos = t0 + lax.broadcasted_iota(jnp.int32, (1, qb), 1)
    qlim = jnp.minimum((lax.shift_right_logical(qpos, int(math.log2(CHUNK))) + 1) * CHUNK, n_keys)
    key_row = lax.broadcasted_iota(jnp.int32, (sl, qb), 0)

    def a_body(j, carry, masked):
        s0 = pl.multiple_of(j * sl, sl)
        kis = ki_ref[pl.ds(s0, sl), 0:HEAD_DIM_IDX]
        s = w_h[0] * jnp.maximum(_dot(kis, qi_h[0]), 0.0)
        for h in range(1, N_HEADS_IDX):
            s = s + w_h[h] * jnp.maximum(_dot(kis, qi_h[h]), 0.0)
        b = lax.bitcast_convert_type(s, jnp.int32)
        skey = jnp.where(b < 0, INT_MIN - b, b)
        if masked:
            skey = jnp.where(s0 + key_row < qlim, skey, INT_MIN)
        key_scr[j] = skey
        half_scr[j] = lax.shift_right_arithmetic(skey, 16).astype(I16)
        return carry

    n_open = jnp.minimum(jnp.minimum((t0 // CHUNK + 1) * CHUNK, n_keys) // sl, nslab)

    def a_pair(jj, carry):
        a_body(2 * jj, carry, False)
        return a_body(2 * jj + 1, carry, False)

    lax.fori_loop(0, n_open // 2, a_pair, 0)
    lax.fori_loop(2 * (n_open // 2), nslab, functools.partial(a_body, masked=True), 0)
    half_scr[nslab] = jnp.full((sl, qb), MIN16, I16)

    def count_ge16(cand16):
        def c_body(jj, acc):
            for j in (2 * jj, 2 * jj + 1):
                hit = jnp.where(half_scr[j] >= cand16, ONE16, ZERO16)
                hit = hit.reshape(sl // (16 * COUNT_CHAINS), COUNT_CHAINS, 16, qb)
                for a in range(hit.shape[0]):
                    acc = acc + hit[a]
            return acc
        acc = lax.fori_loop(0, (nslab + 1) // 2, c_body, jnp.zeros((COUNT_CHAINS, 16, qb), I16))
        tot = acc[0]
        for c in range(1, COUNT_CHAINS):
            tot = tot + acc[c]
        return jnp.sum(tot.astype(F32), axis=0, keepdims=True)

    def radix16(n_above):
        def bit_body(it, carry):
            t_u, n_above = carry
            cand_u = t_u | jnp.left_shift(jnp.int32(1), 15 - it)
            cnt = count_ge16((cand_u - 2 ** 15).astype(I16))
            take = cnt >= ktop
            return jnp.where(take, cand_u, t_u), jnp.where(take, n_above, cnt)
        return lax.fori_loop(0, 16, bit_body, (jnp.zeros((1, qb), jnp.int32), n_above))

    hi_u, n_above_hi = radix16(jnp.zeros((1, qb), F32))
    hi16 = (hi_u - 2 ** 15).astype(I16)

    def low_body(j, carry):
        low = ((key_scr[j] & 0xFFFF) - 2 ** 15).astype(I16)
        high = half_scr[j]
        half_scr[j] = jnp.where(high == hi16, low, jnp.where(high > hi16, MAX16, MIN16))
        return carry

    lax.fori_loop(0, nslab, low_body, 0)
    low_u, n_gt = radix16(n_above_hi)
    thr = (hi_u - 2 ** 15) * 2 ** 16 + low_u
    need = jnp.where(thr == INT_MIN, 0.0, ktop - n_gt)

    tie_scr[...] = jnp.zeros(tie_scr.shape, F32)

    def mask_slab(j):
        keys = key_scr[j]
        eq = keys == thr
        pref = _dot(tri_ref[...], jnp.where(eq, 1.0, 0.0).astype(BF16))
        seen = tie_scr[...]
        sel = (keys > thr) | (eq & (pref + seen <= need))
        key_scr[j] = lax.bitcast_convert_type(jnp.where(sel, 0.0, NEG_BIG), jnp.int32)
        tie_scr[...] = seen + pref[sl - 1:sl]

    jfar = jnp.minimum(jnp.maximum(t0 - near_off + sl - 1, 0) // sl, nslab)
    n_pairs = N_HEADS_A // 2
    qa_t = qa_ref[...].astype(F32).T
    low = lax.broadcasted_iota(jnp.int32, (LANES, qb), 0) < HEAD_DIM_A
    for p in range(n_pairs):
        blk = qa_t[p * LANES:(p + 1) * LANES]
        qt_scr[p, :, 0:qb] = jnp.where(low, blk, 0.0).astype(BF16)
        qt_scr[p, :, qb:2 * qb] = jnp.where(low, 0.0, blk).astype(BF16)
    m_scr[...] = jnp.full(m_scr.shape, NEG_BIG, F32)
    acc_scr[...] = jnp.zeros(acc_scr.shape, F32)

    bufs = ((lga_scr, cma_scr), (lgb_scr, cmb_scr))

    def logits_pair(j, p, buf, near):
        lg_scr, cm_scr = bufs[buf]
        s0 = pl.multiple_of(j * sl, sl)
        madd = lax.bitcast_convert_type(key_scr[j], F32)
        k2 = kb_ref[pl.ds(s0, sl), p * LANES:(p + 1) * LANES]
        l2 = _dot(k2, qt_scr[p])
        for e in range(2):
            h = 2 * p + e
            logit = l2[:, e * qb:(e + 1) * qb] + madd
            if near:
                c0 = near_off - t0 + s0 + qb - LANES + sl
                logit = logit + jnp.concatenate(
                    [bias_ref[h, pl.ds(pl.multiple_of(c0 - g * LANES, LANES), sl), :]
                     for g in range(n_lane_tiles)], axis=1)
            logit = logit.astype(BF16)
            lg_scr[h] = logit
            part = jnp.max(logit.reshape(sl // (16 * FOLD_CHAINS), FOLD_CHAINS, 16, qb), axis=0)
            cm_scr[h] = jnp.max(jnp.max(part, axis=0).astype(F32), axis=0, keepdims=True)

    def softmax_head(j, h, buf):
        lg_scr, cm_scr = bufs[buf]
        m_prev = m_scr[h]
        m_new = jnp.maximum(m_prev, cm_scr[h])
        alpha = jnp.exp2(m_prev - m_new)
        pexp = jnp.exp2(lg_scr[h] - m_new.astype(BF16))
        acc_scr[h] = alpha * acc_scr[h] + _dot(vt_ref[j, h], pexp)
        m_scr[h] = m_new

    def logits_stage(j, buf, near):
        for p in range(n_pairs):
            logits_pair(j, p, buf, near)

    def softmax_stage(j, buf):
        for h in range(N_HEADS_A):
            softmax_head(j, h, buf)

    def overlapped(j_soft, buf_soft, j_logits, near):
        mask_slab(j_logits)
        for p in range(n_pairs):
            logits_pair(j_logits, p, 1 - buf_soft, near)
            softmax_head(j_soft, 2 * p, buf_soft)
            softmax_head(j_soft, 2 * p + 1, buf_soft)

    def d_body(jj, carry, near):
        j = 2 * jj
        overlapped(j, 0, j + 1, near)
        overlapped(j + 1, 1, j + 2, near)
        return carry

    def odd_tail():
        softmax_stage(nslab - 1, 0)

    def even_tail():
        overlapped(nslab - 2, 0, nslab - 1, True)
        softmax_stage(nslab - 1, 1)

    mask_slab(0)
    lax.cond(jfar > 0, lambda: logits_stage(0, 0, False), lambda: logits_stage(0, 0, True))
    n_trips = (nslab - 1) // 2
    far_trips = jnp.minimum(jnp.maximum(jfar - 1, 0) // 2, n_trips)
    lax.fori_loop(0, far_trips, functools.partial(d_body, near=False), 0)
    lax.fori_loop(far_trips, n_trips, functools.partial(d_body, near=True), 0)
    lax.cond(lax.rem(nslab, 2) == 1, odd_tail, even_tail)

    for p in range(n_pairs):
        outs = [acc_scr[h, 0:HEAD_DIM_A] / acc_scr[h, HEAD_DIM_A:HEAD_DIM_A + 1] for h in (2 * p, 2 * p + 1)]
        o_ref[:, p * LANES:(p + 1) * LANES] = jnp.concatenate(outs, axis=0).T.astype(o_ref.dtype)


def _t5_bucket_np(rel):
    nb = NUM_BUCKETS // 2
    max_exact = nb // 2
    ret = np.where(rel > 0, nb, 0)
    n = np.abs(rel)
    nf = np.maximum(n, max_exact).astype(np.float64)
    large = max_exact + (np.log(nf / max_exact) / math.log(MAX_DISTANCE / max_exact) * (nb - max_exact)).astype(np.int64)
    large = np.minimum(large, nb - 1)
    return ret + np.where(n < max_exact, n, large)


def _t5_bucket(rel):
    nb = NUM_BUCKETS // 2
    max_exact = nb // 2
    ret = jnp.where(rel > 0, nb, 0)
    n = jnp.abs(rel)
    nf = jnp.maximum(n, max_exact).astype(F32)
    large = max_exact + (jnp.log(nf / max_exact) / math.log(MAX_DISTANCE / max_exact) * (nb - max_exact)).astype(jnp.int32)
    large = jnp.minimum(large, nb - 1)
    return ret + jnp.where(n < max_exact, n, large)


def _attention(qa, qi, wi, kb, vt, kib, t5_bias, *, qb, sl, n_keys, pos0):
    bsz, tq, _ = qa.shape
    lp = kb.shape[1]
    assert tq % qb == 0 and lp % sl == 0 and sl % LANES == 0 and qb % LANES == 0
    ktop = min(TOPK_MAX, n_keys // 4)
    nslab_max = lp // sl

    gran = math.gcd(pos0, sl) if tq == qb else math.gcd(math.gcd(pos0, qb), sl)
    assert gran % LANES == 0
    rel_all = np.arange(-(n_keys + qb), 0)
    far_bucket = _t5_bucket_np(np.array([-(n_keys + qb)]))[0]
    sat = rel_all[_t5_bucket_np(rel_all) != far_bucket]
    n_sat = int(-sat.min()) + 1 if sat.size else 1
    near_off = -(-(sl - 1 - gran + n_sat) // gran) * gran
    rel_min = -(qb - 1) - near_off
    rels = jnp.arange(rel_min, sl, dtype=jnp.int32)
    tab = (t5_bias[_t5_bucket(rels)] - t5_bias[far_bucket][None, :]) * LOG2_E
    n_rel = sl - rel_min
    n_rows = near_off + sl + qb - LANES
    assert n_rows + LANES - 1 == n_rel
    rolled = jnp.roll(tab.T, -(LANES - 1), axis=1)
    toep = jnp.tile(rolled, (1, LANES))[:, :LANES * (n_rel - 1)].reshape(N_HEADS_A, LANES, n_rel - 1)
    master = jnp.transpose(toep[:, :, :n_rows], (0, 2, 1))
    master = jnp.pad(master, ((0, 0), (sl, 0), (0, 0)))

    tri = jnp.asarray(np.tril(np.ones((sl, sl), np.float32)), dtype=BF16)

    kern = functools.partial(_attn_kernel, qb=qb, sl=sl, n_keys=n_keys, pos0=pos0,
                             near_off=near_off, ktop=float(ktop))

    def qspec(width_):
        return pl.BlockSpec((None, qb, width_), lambda b, i: (b, i, 0))

    def kspec(*shape):
        nd = len(shape)
        return pl.BlockSpec((None,) + shape, lambda b, i: (b,) + (0,) * nd, pipeline_mode=pl.Buffered(1))

    return pl.pallas_call(
        kern,
        grid=(bsz, tq // qb),
        in_specs=[qspec(W_A), qspec(W_IQ), qspec(LANES), kspec(lp, W_A),
                  kspec(nslab_max, N_HEADS_A, VT_ROWS, sl),
                  kspec(lp, LANES), _const_spec(master.shape), _const_spec((sl, sl))],
        out_specs=qspec(W_A),
        out_shape=jax.ShapeDtypeStruct((bsz, tq, W_A), BF16),
        scratch_shapes=[pltpu.VMEM((nslab_max, sl, qb), jnp.int32),
                        pltpu.VMEM((nslab_max + 1, sl, qb), I16),
                        pltpu.VMEM((N_HEADS_A // 2, LANES, 2 * qb), BF16),
                        pltpu.VMEM((N_HEADS_A, 1, qb), F32),
                        pltpu.VMEM((N_HEADS_A, VT_ROWS, qb), F32),
                        pltpu.VMEM((N_HEADS_A, sl, qb), BF16), pltpu.VMEM((N_HEADS_A, 1, qb), F32),
                        pltpu.VMEM((N_HEADS_A, sl, qb), BF16), pltpu.VMEM((N_HEADS_A, 1, qb), F32),
                        pltpu.VMEM((1, qb), F32)],
        compiler_params=pltpu.CompilerParams(
            dimension_semantics=("arbitrary", "arbitrary"), vmem_limit_bytes=VMEM_LIMIT_BYTES),
        name="dsa_attention",
    )(qa, qi, wi, kb, vt, kib, master, tri)


def _ret_kernel(q_ref, k_ref, v_ref, g_ref, gn_ref, s0_ref, dm_ref, qd_ref, kd_ref, gc_ref,
                r_ref, sfin_ref, s_scr, *, n_chunks):
    c = pl.program_id(1)

    @pl.when(c == 0)
    def _():
        s_scr[...] = s0_ref[...]

    for h in range(N_HEADS_R):
        p = h // 2
        q2 = q_ref[:, p * LANES:(p + 1) * LANES].astype(F32)
        k2 = k_ref[:, p * LANES:(p + 1) * LANES]
        vh = v_ref[:, h * VAL_DIM_R:(h + 1) * VAL_DIM_R]
        s_h = s_scr[h]
        qm = (q2 * qd_ref[h, 0]).astype(BF16)
        qdec = (q2 * qd_ref[h, 1]).astype(BF16)
        kdec = (k2.astype(F32) * kd_ref[h]).astype(BF16)
        inner = _dot_nt(qm, k2) * dm_ref[h]
        o = _dot(inner.astype(BF16), vh) + _dot(qdec, s_h.astype(BF16))
        s_scr[h] = gc_ref[h] * s_h + _dot_tn(kdec, vh)
        mu = jnp.mean(o, axis=-1, keepdims=True)
        d = o - mu
        var = jnp.mean(d * d, axis=-1, keepdims=True)
        on = d * lax.rsqrt(var + GN_EPS) * gn_ref[:, h * VAL_DIM_R:(h + 1) * VAL_DIM_R]
        gate = g_ref[:, h * VAL_DIM_R:(h + 1) * VAL_DIM_R].astype(F32)
        r_ref[:, h * VAL_DIM_R:(h + 1) * VAL_DIM_R] = (on * (gate * jax.nn.sigmoid(gate))).astype(r_ref.dtype)

    @pl.when(c == n_chunks - 1)
    def _():
        sfin_ref[...] = s_scr[...]


def _pair_lane_heads():
    return (np.arange(LANES) // (KEY_DIM_R // 2)) % 2


def _retention(qr, kr, vr, gr, gn_g, s_init, log_gamma, *, chunk):
    bsz, t, _ = qr.shape
    n_chunks = t // chunk
    n = jnp.arange(chunk, dtype=F32)
    lg = log_gamma.astype(F32)
    diff = n[:, None] - n[None, :]
    dmask = jnp.where(diff >= 0, jnp.exp(lg[:, None, None] * jnp.maximum(diff, 0.0)), 0.0)
    owner = jnp.asarray(_pair_lane_heads()[None, :] == (np.arange(N_HEADS_R) % 2)[:, None], F32)
    q_dec = jnp.exp(lg[:, None] * (n + 1.0))
    k_dec = jnp.exp(lg[:, None] * (chunk - 1.0 - n))
    qd = jnp.stack([jnp.broadcast_to(owner[:, None, :], (N_HEADS_R, chunk, LANES)),
                    owner[:, None, :] * q_dec[:, :, None]], axis=1)
    kd = owner[:, None, :] * k_dec[:, :, None]
    gc = jnp.broadcast_to(jnp.exp(lg * chunk)[:, None, None], (N_HEADS_R, 1, LANES))

    def tspec(width):
        return pl.BlockSpec((None, chunk, width), lambda b, c: (b, c, 0))

    sspec = pl.BlockSpec((None, N_HEADS_R, LANES, VAL_DIM_R), lambda b, c: (b, 0, 0, 0))
    return pl.pallas_call(
        functools.partial(_ret_kernel, n_chunks=n_chunks),
        grid=(bsz, n_chunks),
        in_specs=[tspec(W_RQK), tspec(W_RQK), tspec(W_RV), tspec(W_RV), _const_spec((1, W_RV)), sspec,
                  _const_spec(dmask.shape), _const_spec(qd.shape), _const_spec(kd.shape),
                  _const_spec(gc.shape)],
        out_specs=(tspec(W_RV), sspec),
        out_shape=(jax.ShapeDtypeStruct((bsz, t, W_RV), BF16),
                   jax.ShapeDtypeStruct((bsz, N_HEADS_R, LANES, VAL_DIM_R), F32)),
        scratch_shapes=[pltpu.VMEM((N_HEADS_R, LANES, VAL_DIM_R), F32)],
        compiler_params=pltpu.CompilerParams(
            dimension_semantics=("arbitrary", "arbitrary"), vmem_limit_bytes=VMEM_LIMIT_BYTES),
        name="retention",
    )(qr, kr, vr, gr, gn_g, s_init, dmask, qd, kd, gc)


def _finish_kernel(x_ref, a_ref, r_ref, ga_ref, gg_ref, wpa, wpr, wo, ln1g, ln1b, wg, wu, wd, ln2g, ln2b,
                   y_ref):
    x = x_ref[...]
    merged = (jax.nn.sigmoid(ga_ref[...].astype(F32)) * _dot(a_ref[...], wpa[...])
              + jax.nn.sigmoid(gg_ref[...].astype(F32)) * _dot(r_ref[...], wpr[...]))
    x1 = _layer_norm(ALPHA * x + _dot(merged.astype(BF16), wo[...]), ln1g[...], ln1b[...])
    x1b = x1.astype(BF16)
    gate = _dot(x1b, wg[...])
    hidden = gate * jax.nn.sigmoid(gate) * _dot(x1b, wu[...])
    y = _layer_norm(ALPHA * x1 + _dot(hidden.astype(BF16), wd[...]), ln2g[...], ln2b[...])
    y_ref[...] = y


def _finish(x2d, a, r, ga, gg, wpa, wpr, wo, ln1g, ln1b, wg, wu, wd, ln2g, ln2b, *, tm):
    m = x2d.shape[0]

    def rspec(width):
        return pl.BlockSpec((tm, width), lambda i: (i, 0))

    consts = (wpa, wpr, wo, ln1g, ln1b, wg, wu, wd, ln2g, ln2b)
    return pl.pallas_call(
        _finish_kernel,
        grid=(m // tm,),
        in_specs=[rspec(D_MODEL), rspec(W_A), rspec(W_RV), rspec(D_MODEL), rspec(D_MODEL)]
        + [_const_spec(c.shape) for c in consts],
        out_specs=rspec(D_MODEL),
        out_shape=jax.ShapeDtypeStruct((m, D_MODEL), F32),
        compiler_params=pltpu.CompilerParams(
            dimension_semantics=("arbitrary",), vmem_limit_bytes=VMEM_LIMIT_BYTES),
        name="finish",
    )(x2d, a, r, ga, gg, *consts)


def _relayout_w_in(w_in):
    offs = np.cumsum((0,) + SPLIT_SIZES)
    parts = [w_in[:, offs[i]:offs[i + 1]] for i in range(len(SPLIT_SIZES))]
    w_qa, w_ka, w_va, w_qi, w_ki, w_wi, w_qr, w_kr, w_vr, w_gr, w_ga, w_gg = parts
    half = KEY_DIM_R // 2
    perm = np.concatenate([
        np.arange(hh * KEY_DIM_R + part * half, hh * KEY_DIM_R + (part + 1) * half)
        for p in range(N_HEADS_R // 2) for part in (0, 1) for hh in (2 * p, 2 * p + 1)])
    w_kw = jnp.pad(jnp.concatenate([w_ki, w_wi], axis=1),
                   ((0, 0), (0, LANES - HEAD_DIM_IDX - N_HEADS_IDX)))
    w_all = jnp.concatenate(
        [w_qa, w_ka, w_va, w_qi, w_kw, w_qr[:, perm], w_kr[:, perm], w_vr, w_gr, w_ga, w_gg], axis=1)
    assert w_all.shape[1] == W_TOTAL
    return w_all.astype(BF16)


def _rotary_tables(pos):
    half = KEY_DIM_R // 2
    inv_freq = ROPE_BASE ** (-jnp.arange(half, dtype=F32) / half)
    ang = pos.astype(F32)[:, None] * inv_freq[None, :]
    cos, sin = jnp.cos(ang), jnp.sin(ang)
    return jnp.tile(cos, (1, 4)), jnp.concatenate([-sin, -sin, sin, sin], axis=1)


def _group(x, pos0, past_k, past_v, past_ki, state, w_all, g_pad, b_pad, t5_bias, log_gamma, fin_w,
           *, tm_proj, qb, sl, ret_chunk, tm_fin):
    bsz, t, _ = x.shape
    m = bsz * t
    x2d = x.reshape(m, D_MODEL)
    pos = pos0 + jnp.arange(t, dtype=jnp.int32)
    cos_t, sin_t = _rotary_tables(pos)
    n_pos_blocks = max(t // tm_proj, 1)
    if tm_proj > t:
        cos_t = jnp.tile(cos_t, (tm_proj // t, 1))
        sin_t = jnp.tile(sin_t, (tm_proj // t, 1))
    emit_vt = past_k is None and tm_proj == sl and t % sl == 0
    (qa, kf, kb, vf, vb, qi, kif, kib, wi, qr, kr, vr, gr, ga, gg) = _proj(
        x2d, w_all, cos_t, sin_t, g_pad, b_pad, tm=tm_proj, n_pos_blocks=n_pos_blocks, emit_vt=emit_vt)

    def b3(a):
        return a.reshape(bsz, t, a.shape[-1])

    n_keys = t if past_k is None else past_k.shape[1] + t
    lp = -(-n_keys // sl) * sl
    kb3, kib3 = b3(kb), b3(kib)
    if emit_vt:
        vt = vb.reshape(bsz, lp // sl, N_HEADS_A, VT_ROWS, sl)
    else:
        vb3 = b3(vb)
        if past_k is not None:
            pk = past_k.reshape(bsz, -1, W_A).astype(BF16)
            pv = past_v.reshape(bsz, -1, W_A).astype(BF16)
            pki = jnp.pad(past_ki, ((0, 0), (0, 0), (0, LANES - HEAD_DIM_IDX))).astype(BF16)
            kb3 = jnp.concatenate([pk, kb3], axis=1)
            vb3 = jnp.concatenate([pv, vb3], axis=1)
            kib3 = jnp.concatenate([pki, kib3], axis=1)
        if lp != n_keys:
            padw = ((0, 0), (0, lp - n_keys), (0, 0))
            kb3, vb3, kib3 = jnp.pad(kb3, padw), jnp.pad(vb3, padw), jnp.pad(kib3, padw)
        vt = vb3.reshape(bsz, lp // sl, sl, W_A).transpose(0, 1, 3, 2)
        vt = vt.reshape(bsz, lp // sl, N_HEADS_A, HEAD_DIM_A, sl)
        vt = jnp.pad(vt, ((0, 0), (0, 0), (0, 0), (0, VT_ROWS - HEAD_DIM_A), (0, 0)), constant_values=1)
    qa3, qi3, wi3 = b3(qa), b3(qi), b3(wi)
    tq = -(-t // qb) * qb
    if tq != t:
        padq = ((0, 0), (0, tq - t), (0, 0))
        qa3, qi3, wi3 = jnp.pad(qa3, padq), jnp.pad(qi3, padq), jnp.pad(wi3, padq)
    a = _attention(qa3, qi3, wi3, kb3, vt, kib3, t5_bias, qb=qb, sl=sl, n_keys=n_keys, pos0=pos0)[:, :t]

    half = KEY_DIM_R // 2
    npair = N_HEADS_R // 2
    if state is None:
        s_init = jnp.zeros((bsz, N_HEADS_R, LANES, VAL_DIM_R), F32)
    else:
        st = state.astype(F32).reshape(bsz, npair, 2, 2, half, VAL_DIM_R)
        zero = jnp.zeros_like(st[:, :, 0])
        s_init = jnp.stack([jnp.stack([st[:, :, 0], zero], axis=3),
                            jnp.stack([zero, st[:, :, 1]], axis=3)], axis=2)
        s_init = s_init.reshape(bsz, N_HEADS_R, LANES, VAL_DIM_R)
    r, s_fin = _retention(b3(qr), b3(kr), b3(vr), b3(gr), fin_w["gn"], s_init, log_gamma, chunk=ret_chunk)
    s6 = s_fin.reshape(bsz, npair, 2, 2, 2, half, VAL_DIM_R)
    s_out = jnp.stack([s6[:, :, 0, :, 0], s6[:, :, 1, :, 1]], axis=2)
    s_out = s_out.reshape(bsz, N_HEADS_R, KEY_DIM_R, VAL_DIM_R)

    y = _finish(x2d, a.reshape(m, W_A), r.reshape(m, W_RV), ga, gg,
                fin_w["wpa"], fin_w["wpr"], fin_w["wo"], fin_w["ln1g"], fin_w["ln1b"],
                fin_w["wg"], fin_w["wu"], fin_w["wd"], fin_w["ln2g"], fin_w["ln2b"], tm=tm_fin)
    return (y.reshape(bsz, t, D_MODEL),
            kf.reshape(bsz, t, N_HEADS_A, HEAD_DIM_A), vf.reshape(bsz, t, N_HEADS_A, HEAD_DIM_A),
            kif.reshape(bsz, t, HEAD_DIM_IDX), s_out)


def kernel(x_prompt, x_sample, cache_k, cache_v, cache_idx_k, state_ret, w_in, idx_k_norm_g, idx_k_norm_b,
           t5_bias, ret_gn_g, w_pa, w_pr, w_o, ln1_g, ln1_b, w_gate, w_up, w_down, ln2_g, ln2_b):
    assert w_in.shape[0] == DEPTH
    log_gamma = jnp.log1p(-jnp.exp2(-5.0 - jnp.arange(N_HEADS_R, dtype=F32)))
    l = 0
    w_all = _relayout_w_in(w_in[l])
    lane_pad = (0, LANES - HEAD_DIM_IDX)
    g_pad = jnp.pad(idx_k_norm_g[l].astype(F32), lane_pad)[None, :]
    b_pad = jnp.pad(idx_k_norm_b[l].astype(F32), lane_pad)[None, :]
    fin_w = dict(
        gn=ret_gn_g[l][None, :].astype(F32),
        wpa=w_pa[l].astype(BF16), wpr=w_pr[l].astype(BF16), wo=w_o[l].astype(BF16),
        ln1g=ln1_g[l][None, :].astype(F32), ln1b=ln1_b[l][None, :].astype(F32),
        wg=w_gate[l].astype(BF16), wu=w_up[l].astype(BF16), wd=w_down[l].astype(BF16),
        ln2g=ln2_g[l][None, :].astype(F32), ln2b=ln2_b[l][None, :].astype(F32))

    t_p = x_prompt.shape[1]
    yp, kp, vp, ikp, sp = _group(
        x_prompt, 0, None, None, None, None, w_all, g_pad, b_pad, t5_bias, log_gamma, fin_w,
        tm_proj=min(512, t_p), qb=min(256, t_p), sl=512, ret_chunk=min(256, t_p), tm_fin=min(512, t_p))

    bs, t_s, _ = x_sample.shape
    past = cache_k.shape[2]
    ys, ks, vs, iks, ss = _group(
        x_sample, past, cache_k[l], cache_v[l], cache_idx_k[l], state_ret[l], w_all, g_pad, b_pad, t5_bias,
        log_gamma, fin_w, tm_proj=bs * t_s, qb=LANES, sl=512, ret_chunk=t_s, tm_fin=bs * t_s)

    return (yp, ys, kp[None], vp[None], ikp[None], sp[None],
            ks[None], vs[None], iks[None], ss.astype(state_ret.dtype)[None])
```

```python
import functools
import math

import numpy as np
import jax
import jax.numpy as jnp
from jax import lax
from jax.experimental import pallas as pl
from jax.experimental.pallas import tpu as pltpu

D_MODEL = 1024
CHUNK = 64
N_HEADS_A = 8
HEAD_DIM_A = 64
N_HEADS_IDX = 4
HEAD_DIM_IDX = 64
TOPK_MAX = 256
NUM_BUCKETS = 32
MAX_DISTANCE = 1024
N_HEADS_R = 8
KEY_DIM_R = 64
VAL_DIM_R = 128
ROPE_BASE = 10000.0
D_FF = 2816
DEPTH = 1
ALPHA = (2.0 * DEPTH) ** 0.25
LN_EPS = 1e-5
GN_EPS = 1e-6

W_A = N_HEADS_A * HEAD_DIM_A
W_IQ = N_HEADS_IDX * HEAD_DIM_IDX
W_RQK = N_HEADS_R * KEY_DIM_R
W_RV = N_HEADS_R * VAL_DIM_R
SPLIT_SIZES = (W_A, W_A, W_A, W_IQ, HEAD_DIM_IDX, N_HEADS_IDX, W_RQK, W_RQK, W_RV, W_RV, D_MODEL, D_MODEL)

LANES = 128
VMEM_LIMIT_BYTES = 60 * 1024 * 1024
DOUBLE_BUFFER_VMEM_SHARE = 0.75

OFF_QA = 0
OFF_KA = OFF_QA + W_A
OFF_VA = OFF_KA + W_A
OFF_QI = OFF_VA + W_A
OFF_KW = OFF_QI + W_IQ
OFF_QR = OFF_KW + LANES
OFF_KR = OFF_QR + W_RQK
OFF_VR = OFF_KR + W_RQK
OFF_GR = OFF_VR + W_RV
OFF_GA = OFF_GR + W_RV
OFF_GG = OFF_GA + D_MODEL
W_TOTAL = OFF_GG + D_MODEL

INT_MIN = -(2 ** 31)
NEG_BIG = -1e30
LOG2_E = math.log2(math.e)
FOLD_CHAINS = 8
COUNT_CHAINS = 2
VT_ROWS = HEAD_DIM_A + 16
BF16 = jnp.bfloat16
F32 = jnp.float32
I16 = jnp.int16
ONE16, ZERO16 = np.int16(1), np.int16(0)
MAX16, MIN16 = np.int16(2 ** 15 - 1), np.int16(-(2 ** 15))


def _dot(a, b):
    return jnp.dot(a, b, preferred_element_type=F32)


def _dot_nt(a, b):
    return lax.dot_general(a, b, (((1,), (1,)), ((), ())), preferred_element_type=F32)


def _dot_tn(a, b):
    return lax.dot_general(a, b, (((0,), (0,)), ((), ())), preferred_element_type=F32)


def _layer_norm(z, g, b):
    mu = jnp.mean(z, axis=-1, keepdims=True)
    d = z - mu
    var = jnp.mean(d * d, axis=-1, keepdims=True)
    return d * lax.rsqrt(var + LN_EPS) * g + b


def _const_spec(shape):
    nd = len(shape)
    return pl.BlockSpec(shape, lambda *_: (0,) * nd, pipeline_mode=pl.Buffered(1))


def _proj_kernel(x_ref, w_ref, cos_ref, sin_ref, g_ref, b_ref,
                 qa_o, kf_o, kb_o, vf_o, vb_o, qi_o, kif_o, kib_o, wi_o,
                 qr_o, kr_o, vr_o, gr_o, ga_o, gg_o, *, emit_vt):
    xb = x_ref[...].astype(BF16)

    def seg(off, n):
        return _dot(xb, w_ref[:, off:off + n])

    qa_o[...] = (seg(OFF_QA, W_A) * (HEAD_DIM_A ** -0.5 * LOG2_E)).astype(BF16)
    k = seg(OFF_KA, W_A)
    kf_o[...] = k.reshape(kf_o.shape)
    kb_o[...] = k.astype(BF16)
    v = seg(OFF_VA, W_A)
    vf_o[...] = v.reshape(vf_o.shape)
    if emit_vt:
        v_t = v.T
        for h in range(N_HEADS_A):
            vb_o[h, 0:HEAD_DIM_A, :] = v_t[h * HEAD_DIM_A:(h + 1) * HEAD_DIM_A].astype(BF16)
            vb_o[h, HEAD_DIM_A:VT_ROWS, :] = jnp.ones((VT_ROWS - HEAD_DIM_A, v_t.shape[1]), BF16)
    else:
        vb_o[...] = v.astype(BF16)
    qi_o[...] = seg(OFF_QI, W_IQ).astype(BF16)

    kw = seg(OFF_KW, LANES)
    first = lax.broadcasted_iota(jnp.int32, kw.shape, 1) < HEAD_DIM_IDX
    inv_n = 1.0 / HEAD_DIM_IDX
    mu = jnp.sum(jnp.where(first, kw, 0.0), axis=-1, keepdims=True) * inv_n
    d = kw - mu
    var = jnp.sum(jnp.where(first, d * d, 0.0), axis=-1, keepdims=True) * inv_n
    kin = d * lax.rsqrt(var + LN_EPS) * g_ref[...] + b_ref[...]
    kif_o[...] = kin[:, :HEAD_DIM_IDX]
    kib_o[...] = kin.astype(BF16)

    wi_o[...] = kw * ((N_HEADS_IDX ** -0.5) * (HEAD_DIM_IDX ** -0.5))

    for off, o_ref, scale in ((OFF_QR, qr_o, 1.0), (OFF_KR, kr_o, KEY_DIM_R ** -0.5)):
        h = seg(off, W_RQK)
        for j in range(W_RQK // LANES):
            xj = h[:, j * LANES:(j + 1) * LANES]
            rj = pltpu.roll(xj, LANES // 2, 1)
            oj = xj * cos_ref[...] + rj * sin_ref[...]
            o_ref[:, j * LANES:(j + 1) * LANES] = (oj * scale).astype(BF16)

    vr_o[...] = seg(OFF_VR, W_RV).astype(BF16)
    gr_o[...] = seg(OFF_GR, W_RV).astype(BF16)
    ga_o[...] = seg(OFF_GA, D_MODEL).astype(BF16)
    gg_o[...] = seg(OFF_GG, D_MODEL).astype(BF16)


def _proj(x2d, w_all, cos_t, sin_t, g_pad, b_pad, *, tm, n_pos_blocks, emit_vt):
    m = x2d.shape[0]
    grid = (m // tm,)

    def row(width, dtype):
        return jax.ShapeDtypeStruct((m, width), dtype)

    def rspec(width):
        return pl.BlockSpec((tm, width), lambda i: (i, 0))

    pos_spec = pl.BlockSpec((tm, LANES), lambda i: (i % n_pos_blocks, 0))
    out_shapes = [
        row(W_A, BF16), row(W_A, F32), row(W_A, BF16), row(W_A, F32), row(W_A, BF16),
        row(W_IQ, BF16), row(HEAD_DIM_IDX, F32), row(LANES, BF16), row(LANES, F32),
        row(W_RQK, BF16), row(W_RQK, BF16), row(W_RV, BF16), row(W_RV, BF16),
        row(D_MODEL, BF16), row(D_MODEL, BF16)]
    out_specs = [rspec(s.shape[1]) for s in out_shapes]
    for idx in (1, 3):
        out_shapes[idx] = jax.ShapeDtypeStruct((m, N_HEADS_A, HEAD_DIM_A), F32)
        out_specs[idx] = pl.BlockSpec((tm, N_HEADS_A, HEAD_DIM_A), lambda i: (i, 0, 0))
    if emit_vt:
        out_shapes[4] = jax.ShapeDtypeStruct((m // tm, N_HEADS_A, VT_ROWS, tm), BF16)
        out_specs[4] = pl.BlockSpec((None, N_HEADS_A, VT_ROWS, tm), lambda i: (i, 0, 0, 0))
    return pl.pallas_call(
        functools.partial(_proj_kernel, emit_vt=emit_vt),
        grid=grid,
        in_specs=[rspec(D_MODEL), _const_spec((D_MODEL, W_TOTAL)), pos_spec, pos_spec,
                  _const_spec((1, LANES)), _const_spec((1, LANES))],
        out_specs=tuple(out_specs),
        out_shape=tuple(out_shapes),
        compiler_params=pltpu.CompilerParams(
            dimension_semantics=("arbitrary",), vmem_limit_bytes=VMEM_LIMIT_BYTES),
        name="proj",
    )(x2d, w_all, cos_t, sin_t, g_pad, b_pad)


def _attn_kernel(qa_ref, qi_ref, wi_ref, kb_ref, vt_ref, ki_ref, bias_ref, tri_ref, o_ref,
                 key_scr, half_scr, qt_scr, m_scr, acc_scr, lga_scr, cma_scr, lgb_scr, cmb_scr, tie_scr,
                 *, qb, sl, n_keys, pos0, near_off, ktop):
    i = pl.program_id(1)
    t0 = pos0 + i * qb
    lblk = jnp.minimum(n_keys, ((t0 + qb - 1) // CHUNK + 1) * CHUNK)
    nslab = (lblk + sl - 1) // sl
    n_lane_tiles = qb // LANES

    qi_t = qi_ref[...].astype(F32).T
    qi_h = [qi_t[h * HEAD_DIM_IDX:(h + 1) * HEAD_DIM_IDX].astype(BF16) for h in range(N_HEADS_IDX)]
    w_t = wi_ref[...].T
    w_h = [w_t[HEAD_DIM_IDX + h:HEAD_DIM_IDX + h + 1] for h in range(N_HEADS_IDX)]
    qpos = t0 + lax.broadcasted_iota(jnp.int32, (1, qb), 1)
    qlim = jnp.minimum((lax.shift_right_logical(qpos, int(math.log2(CHUNK))) + 1) * CHUNK, n_keys)
    key_row = lax.broadcasted_iota(jnp.int32, (sl, qb), 0)

    def a_body(j, carry, masked):
        s0 = pl.multiple_of(j * sl, sl)
        kis = ki_ref[pl.ds(s0, sl), 0:HEAD_DIM_IDX]
        s = w_h[0] * jnp.maximum(_dot(kis, qi_h[0]), 0.0)
        for h in range(1, N_HEADS_IDX):
            s = s + w_h[h] * jnp.maximum(_dot(kis, qi_h[h]), 0.0)
        b = lax.bitcast_convert_type(s, jnp.int32)
        skey = jnp.where(b < 0, INT_MIN - b, b)
        if masked:
            skey = jnp.where(s0 + key_row < qlim, skey, INT_MIN)
        key_scr[j] = skey
        half_scr[j] = lax.shift_right_arithmetic(skey, 16).astype(I16)
        return carry

    n_open = jnp.minimum(jnp.minimum((t0 // CHUNK + 1) * CHUNK, n_keys) // sl, nslab)

    def a_pair(jj, carry):
        a_body(2 * jj, carry, False)
        return a_body(2 * jj + 1, carry, False)

    lax.fori_loop(0, n_open // 2, a_pair, 0)
    lax.fori_loop(2 * (n_open // 2), nslab, functools.partial(a_body, masked=True), 0)
    half_scr[nslab] = jnp.full((sl, qb), MIN16, I16)

    def count_ge16(cand16):
        def c_body(jj, acc):
            for j in (2 * jj, 2 * jj + 1):
                hit = jnp.where(half_scr[j] >= cand16, ONE16, ZERO16)
                hit = hit.reshape(sl // (16 * COUNT_CHAINS), COUNT_CHAINS, 16, qb)
                for a in range(hit.shape[0]):
                    acc = acc + hit[a]
            return acc
        acc = lax.fori_loop(0, (nslab + 1) // 2, c_body, jnp.zeros((COUNT_CHAINS, 16, qb), I16))
        return jnp.sum(acc.astype(F32).reshape(COUNT_CHAINS * 16, qb), axis=0, keepdims=True)

    def radix16(n_above):
        def bit_body(it, carry):
            t_u, n_above = carry
            cand_u = t_u | jnp.left_shift(jnp.int32(1), 15 - it)
            cnt = count_ge16((cand_u - 2 ** 15).astype(I16))
            take = cnt >= ktop
            return jnp.where(take, cand_u, t_u), jnp.where(take, n_above, cnt)
        return lax.fori_loop(0, 16, bit_body, (jnp.zeros((1, qb), jnp.int32), n_above))

    hi_u, n_above_hi = radix16(jnp.zeros((1, qb), F32))
    hi16 = (hi_u - 2 ** 15).astype(I16)

    def low_body(j, carry):
        low = ((key_scr[j] & 0xFFFF) - 2 ** 15).astype(I16)
        high = half_scr[j]
        half_scr[j] = jnp.where(high == hi16, low, jnp.where(high > hi16, MAX16, MIN16))
        return carry

    lax.fori_loop(0, nslab, low_body, 0)
    low_u, n_gt = radix16(n_above_hi)
    thr = (hi_u - 2 ** 15) * 2 ** 16 + low_u
    need = jnp.where(thr == INT_MIN, 0.0, ktop - n_gt)

    tie_scr[...] = jnp.zeros(tie_scr.shape, F32)

    def mask_slab(j):
        keys = key_scr[j]
        eq = keys == thr
        pref = _dot(tri_ref[...], jnp.where(eq, 1.0, 0.0).astype(BF16))
        seen = tie_scr[...]
        sel = (keys > thr) | (eq & (pref + seen <= need))
        key_scr[j] = lax.bitcast_convert_type(jnp.where(sel, 0.0, NEG_BIG), jnp.int32)
        tie_scr[...] = seen + pref[sl - 1:sl]

    jfar = jnp.minimum(jnp.maximum(t0 - near_off + sl - 1, 0) // sl, nslab)
    n_pairs = N_HEADS_A // 2
    qa_t = qa_ref[...].astype(F32).T
    low = lax.broadcasted_iota(jnp.int32, (LANES, qb), 0) < HEAD_DIM_A
    for p in range(n_pairs):
        blk = qa_t[p * LANES:(p + 1) * LANES]
        qt_scr[p, :, 0:qb] = jnp.where(low, blk, 0.0).astype(BF16)
        qt_scr[p, :, qb:2 * qb] = jnp.where(low, 0.0, blk).astype(BF16)
    m_scr[...] = jnp.full(m_scr.shape, NEG_BIG, F32)
    acc_scr[...] = jnp.zeros(acc_scr.shape, F32)

    bufs = ((lga_scr, cma_scr), (lgb_scr, cmb_scr))

    def logits_pair(j, p, buf, near):
        lg_scr, cm_scr = bufs[buf]
        s0 = pl.multiple_of(j * sl, sl)
        madd = lax.bitcast_convert_type(key_scr[j], F32)
        k2 = kb_ref[pl.ds(s0, sl), p * LANES:(p + 1) * LANES]
        l2 = _dot(k2, qt_scr[p])
        for e in range(2):
            h = 2 * p + e
            logit = l2[:, e * qb:(e + 1) * qb] + madd
            if near:
                c0 = near_off - t0 + s0 + qb - LANES + sl
                logit = logit + jnp.concatenate(
                    [bias_ref[h, pl.ds(pl.multiple_of(c0 - g * LANES, LANES), sl), :]
                     for g in range(n_lane_tiles)], axis=1)
            logit = logit.astype(BF16)
            lg_scr[h] = logit
            part = jnp.max(logit.reshape(sl // (16 * FOLD_CHAINS), FOLD_CHAINS, 16, qb), axis=0)
            cm_scr[h] = jnp.max(jnp.max(part, axis=0).astype(F32), axis=0, keepdims=True)

    def softmax_head(j, h, buf):
        lg_scr, cm_scr = bufs[buf]
        m_prev = m_scr[h]
        m_new = jnp.maximum(m_prev, cm_scr[h])
        alpha = jnp.exp2(m_prev - m_new)
        pexp = jnp.exp2(lg_scr[h] - m_new.astype(BF16))
        acc_scr[h] = alpha * acc_scr[h] + _dot(vt_ref[j, h], pexp)
        m_scr[h] = m_new

    def logits_stage(j, buf, near):
        for p in range(n_pairs):
            logits_pair(j, p, buf, near)

    def softmax_stage(j, buf):
        for h in range(N_HEADS_A):
            softmax_head(j, h, buf)

    def overlapped(j_soft, buf_soft, j_logits, near):
        mask_slab(j_logits)
        for p in range(n_pairs):
            logits_pair(j_logits, p, 1 - buf_soft, near)
            softmax_head(j_soft, 2 * p, buf_soft)
            softmax_head(j_soft, 2 * p + 1, buf_soft)

    def d_body(jj, carry, near):
        j = 2 * jj
        overlapped(j, 0, j + 1, near)
        overlapped(j + 1, 1, j + 2, near)
        return carry

    def odd_tail():
        softmax_stage(nslab - 1, 0)

    def even_tail():
        overlapped(nslab - 2, 0, nslab - 1, True)
        softmax_stage(nslab - 1, 1)

    mask_slab(0)
    lax.cond(jfar > 0, lambda: logits_stage(0, 0, False), lambda: logits_stage(0, 0, True))
    n_trips = (nslab - 1) // 2
    far_trips = jnp.minimum(jnp.maximum(jfar - 1, 0) // 2, n_trips)
    lax.fori_loop(0, far_trips, functools.partial(d_body, near=False), 0)
    lax.fori_loop(far_trips, n_trips, functools.partial(d_body, near=True), 0)
    lax.cond(lax.rem(nslab, 2) == 1, odd_tail, even_tail)

    for p in range(n_pairs):
        outs = [acc_scr[h, 0:HEAD_DIM_A] / acc_scr[h, HEAD_DIM_A:HEAD_DIM_A + 1] for h in (2 * p, 2 * p + 1)]
        o_ref[:, p * LANES:(p + 1) * LANES] = jnp.concatenate(outs, axis=0).T.astype(o_ref.dtype)


def _t5_bucket_np(rel):
    nb = NUM_BUCKETS // 2
    max_exact = nb // 2
    ret = np.where(rel > 0, nb, 0)
    n = np.abs(rel)
    nf = np.maximum(n, max_exact).astype(np.float64)
    large = max_exact + (np.log(nf / max_exact) / math.log(MAX_DISTANCE / max_exact) * (nb - max_exact)).astype(np.int64)
    large = np.minimum(large, nb - 1)
    return ret + np.where(n < max_exact, n, large)


def _t5_bucket(rel):
    nb = NUM_BUCKETS // 2
    max_exact = nb // 2
    ret = jnp.where(rel > 0, nb, 0)
    n = jnp.abs(rel)
    nf = jnp.maximum(n, max_exact).astype(F32)
    large = max_exact + (jnp.log(nf / max_exact) / math.log(MAX_DISTANCE / max_exact) * (nb - max_exact)).astype(jnp.int32)
    large = jnp.minimum(large, nb - 1)
    return ret + jnp.where(n < max_exact, n, large)


def _attention(qa, qi, wi, kb, vt, kib, t5_bias, *, qb, sl, n_keys, pos0):
    bsz, tq, _ = qa.shape
    lp = kb.shape[1]
    assert tq % qb == 0 and lp % sl == 0 and sl % LANES == 0 and qb % LANES == 0
    ktop = min(TOPK_MAX, n_keys // 4)
    nslab_max = lp // sl

    gran = math.gcd(pos0, sl) if tq == qb else math.gcd(math.gcd(pos0, qb), sl)
    assert gran % LANES == 0
    rel_all = np.arange(-(n_keys + qb), 0)
    far_bucket = _t5_bucket_np(np.array([-(n_keys + qb)]))[0]
    sat = rel_all[_t5_bucket_np(rel_all) != far_bucket]
    n_sat = int(-sat.min()) + 1 if sat.size else 1
    near_off = -(-(sl - 1 - gran + n_sat) // gran) * gran
    rel_min = -(qb - 1) - near_off
    rels = jnp.arange(rel_min, sl, dtype=jnp.int32)
    tab = (t5_bias[_t5_bucket(rels)] - t5_bias[far_bucket][None, :]) * LOG2_E
    n_rel = sl - rel_min
    n_rows = near_off + sl + qb - LANES
    assert n_rows + LANES - 1 == n_rel
    rolled = jnp.roll(tab.T, -(LANES - 1), axis=1)
    toep = jnp.tile(rolled, (1, LANES))[:, :LANES * (n_rel - 1)].reshape(N_HEADS_A, LANES, n_rel - 1)
    master = jnp.transpose(toep[:, :, :n_rows], (0, 2, 1))
    master = jnp.pad(master, ((0, 0), (sl, 0), (0, 0)))

    tri = jnp.asarray(np.tril(np.ones((sl, sl), np.float32)), dtype=BF16)

    kern = functools.partial(_attn_kernel, qb=qb, sl=sl, n_keys=n_keys, pos0=pos0,
                             near_off=near_off, ktop=float(ktop))

    def qspec(width_):
        return pl.BlockSpec((None, qb, width_), lambda b, i: (b, i, 0))

    resident = 2 * (lp * W_A + nslab_max * N_HEADS_A * VT_ROWS * sl + lp * LANES)
    scratch = (nslab_max * sl * qb * 4 + (nslab_max + 1) * sl * qb * 2 + 2 * N_HEADS_A * sl * qb * 2
               + master.size * 4 + sl * sl * 2)
    batch_buffers = 2 if 2 * resident + scratch <= DOUBLE_BUFFER_VMEM_SHARE * VMEM_LIMIT_BYTES else 1

    def kspec(*shape):
        nd = len(shape)
        return pl.BlockSpec((None,) + shape, lambda b, i: (b,) + (0,) * nd,
                            pipeline_mode=pl.Buffered(batch_buffers))

    return pl.pallas_call(
        kern,
        grid=(bsz, tq // qb),
        in_specs=[qspec(W_A), qspec(W_IQ), qspec(LANES), kspec(lp, W_A),
                  kspec(nslab_max, N_HEADS_A, VT_ROWS, sl),
                  kspec(lp, LANES), _const_spec(master.shape), _const_spec((sl, sl))],
        out_specs=qspec(W_A),
        out_shape=jax.ShapeDtypeStruct((bsz, tq, W_A), BF16),
        scratch_shapes=[pltpu.VMEM((nslab_max, sl, qb), jnp.int32),
                        pltpu.VMEM((nslab_max + 1, sl, qb), I16),
                        pltpu.VMEM((N_HEADS_A // 2, LANES, 2 * qb), BF16),
                        pltpu.VMEM((N_HEADS_A, 1, qb), F32),
                        pltpu.VMEM((N_HEADS_A, VT_ROWS, qb), F32),
                        pltpu.VMEM((N_HEADS_A, sl, qb), BF16), pltpu.VMEM((N_HEADS_A, 1, qb), F32),
                        pltpu.VMEM((N_HEADS_A, sl, qb), BF16), pltpu.VMEM((N_HEADS_A, 1, qb), F32),
                        pltpu.VMEM((1, qb), F32)],
        compiler_params=pltpu.CompilerParams(
            dimension_semantics=("arbitrary", "arbitrary"), vmem_limit_bytes=VMEM_LIMIT_BYTES),
        name="dsa_attention",
    )(qa, qi, wi, kb, vt, kib, master, tri)


def _ret_kernel(q_ref, k_ref, v_ref, g_ref, gn_ref, s0_ref, dm_ref, qd_ref, kd_ref, gc_ref,
                r_ref, sfin_ref, s_scr, *, n_chunks):
    c = pl.program_id(1)

    @pl.when(c == 0)
    def _():
        s_scr[...] = s0_ref[...]

    for h in range(N_HEADS_R):
        p = h // 2
        q2 = q_ref[:, p * LANES:(p + 1) * LANES].astype(F32)
        k2 = k_ref[:, p * LANES:(p + 1) * LANES]
        vh = v_ref[:, h * VAL_DIM_R:(h + 1) * VAL_DIM_R]
        s_h = s_scr[h]
        qm = (q2 * qd_ref[h, 0]).astype(BF16)
        qdec = (q2 * qd_ref[h, 1]).astype(BF16)
        kdec = (k2.astype(F32) * kd_ref[h]).astype(BF16)
        inner = _dot_nt(qm, k2) * dm_ref[h]
        o = _dot(inner.astype(BF16), vh) + _dot(qdec, s_h.astype(BF16))
        s_scr[h] = gc_ref[h] * s_h + _dot_tn(kdec, vh)
        mu = jnp.mean(o, axis=-1, keepdims=True)
        d = o - mu
        var = jnp.mean(d * d, axis=-1, keepdims=True)
        on = d * lax.rsqrt(var + GN_EPS) * gn_ref[:, h * VAL_DIM_R:(h + 1) * VAL_DIM_R]
        gate = g_ref[:, h * VAL_DIM_R:(h + 1) * VAL_DIM_R].astype(F32)
        r_ref[:, h * VAL_DIM_R:(h + 1) * VAL_DIM_R] = (on * (gate * jax.nn.sigmoid(gate))).astype(r_ref.dtype)

    @pl.when(c == n_chunks - 1)
    def _():
        sfin_ref[...] = s_scr[...]


def _pair_lane_heads():
    return (np.arange(LANES) // (KEY_DIM_R // 2)) % 2


def _retention(qr, kr, vr, gr, gn_g, s_init, log_gamma, *, chunk):
    bsz, t, _ = qr.shape
    n_chunks = t // chunk
    n = jnp.arange(chunk, dtype=F32)
    lg = log_gamma.astype(F32)
    diff = n[:, None] - n[None, :]
    dmask = jnp.where(diff >= 0, jnp.exp(lg[:, None, None] * jnp.maximum(diff, 0.0)), 0.0)
    owner = jnp.asarray(_pair_lane_heads()[None, :] == (np.arange(N_HEADS_R) % 2)[:, None], F32)
    q_dec = jnp.exp(lg[:, None] * (n + 1.0))
    k_dec = jnp.exp(lg[:, None] * (chunk - 1.0 - n))
    qd = jnp.stack([jnp.broadcast_to(owner[:, None, :], (N_HEADS_R, chunk, LANES)),
                    owner[:, None, :] * q_dec[:, :, None]], axis=1)
    kd = owner[:, None, :] * k_dec[:, :, None]
    gc = jnp.broadcast_to(jnp.exp(lg * chunk)[:, None, None], (N_HEADS_R, 1, LANES))

    def tspec(width):
        return pl.BlockSpec((None, chunk, width), lambda b, c: (b, c, 0))

    sspec = pl.BlockSpec((None, N_HEADS_R, LANES, VAL_DIM_R), lambda b, c: (b, 0, 0, 0))
    return pl.pallas_call(
        functools.partial(_ret_kernel, n_chunks=n_chunks),
        grid=(bsz, n_chunks),
        in_specs=[tspec(W_RQK), tspec(W_RQK), tspec(W_RV), tspec(W_RV), _const_spec((1, W_RV)), sspec,
                  _const_spec(dmask.shape), _const_spec(qd.shape), _const_spec(kd.shape),
                  _const_spec(gc.shape)],
        out_specs=(tspec(W_RV), sspec),
        out_shape=(jax.ShapeDtypeStruct((bsz, t, W_RV), BF16),
                   jax.ShapeDtypeStruct((bsz, N_HEADS_R, LANES, VAL_DIM_R), F32)),
        scratch_shapes=[pltpu.VMEM((N_HEADS_R, LANES, VAL_DIM_R), F32)],
        compiler_params=pltpu.CompilerParams(
            dimension_semantics=("arbitrary", "arbitrary"), vmem_limit_bytes=VMEM_LIMIT_BYTES),
        name="retention",
    )(qr, kr, vr, gr, gn_g, s_init, dmask, qd, kd, gc)


def _finish_kernel(x_ref, a_ref, r_ref, ga_ref, gg_ref, wpa, wpr, wo, ln1g, ln1b, wg, wu, wd, ln2g, ln2b,
                   y_ref):
    x = x_ref[...]
    merged = (jax.nn.sigmoid(ga_ref[...].astype(F32)) * _dot(a_ref[...], wpa[...])
              + jax.nn.sigmoid(gg_ref[...].astype(F32)) * _dot(r_ref[...], wpr[...]))
    x1 = _layer_norm(ALPHA * x + _dot(merged.astype(BF16), wo[...]), ln1g[...], ln1b[...])
    x1b = x1.astype(BF16)
    gate = _dot(x1b, wg[...])
    hidden = gate * jax.nn.sigmoid(gate) * _dot(x1b, wu[...])
    y = _layer_norm(ALPHA * x1 + _dot(hidden.astype(BF16), wd[...]), ln2g[...], ln2b[...])
    y_ref[...] = y


def _finish(x2d, a, r, ga, gg, wpa, wpr, wo, ln1g, ln1b, wg, wu, wd, ln2g, ln2b, *, tm):
    m = x2d.shape[0]

    def rspec(width):
        return pl.BlockSpec((tm, width), lambda i: (i, 0))

    consts = (wpa, wpr, wo, ln1g, ln1b, wg, wu, wd, ln2g, ln2b)
    return pl.pallas_call(
        _finish_kernel,
        grid=(m // tm,),
        in_specs=[rspec(D_MODEL), rspec(W_A), rspec(W_RV), rspec(D_MODEL), rspec(D_MODEL)]
        + [_const_spec(c.shape) for c in consts],
        out_specs=rspec(D_MODEL),
        out_shape=jax.ShapeDtypeStruct((m, D_MODEL), F32),
        compiler_params=pltpu.CompilerParams(
            dimension_semantics=("arbitrary",), vmem_limit_bytes=VMEM_LIMIT_BYTES),
        name="finish",
    )(x2d, a, r, ga, gg, *consts)


def _relayout_w_in(w_in):
    offs = np.cumsum((0,) + SPLIT_SIZES)
    parts = [w_in[:, offs[i]:offs[i + 1]] for i in range(len(SPLIT_SIZES))]
    w_qa, w_ka, w_va, w_qi, w_ki, w_wi, w_qr, w_kr, w_vr, w_gr, w_ga, w_gg = parts
    half = KEY_DIM_R // 2
    perm = np.concatenate([
        np.arange(hh * KEY_DIM_R + part * half, hh * KEY_DIM_R + (part + 1) * half)
        for p in range(N_HEADS_R // 2) for part in (0, 1) for hh in (2 * p, 2 * p + 1)])
    w_kw = jnp.pad(jnp.concatenate([w_ki, w_wi], axis=1),
                   ((0, 0), (0, LANES - HEAD_DIM_IDX - N_HEADS_IDX)))
    w_all = jnp.concatenate(
        [w_qa, w_ka, w_va, w_qi, w_kw, w_qr[:, perm], w_kr[:, perm], w_vr, w_gr, w_ga, w_gg], axis=1)
    assert w_all.shape[1] == W_TOTAL
    return w_all.astype(BF16)


def _rotary_tables(pos):
    half = KEY_DIM_R // 2
    inv_freq = ROPE_BASE ** (-jnp.arange(half, dtype=F32) / half)
    ang = pos.astype(F32)[:, None] * inv_freq[None, :]
    cos, sin = jnp.cos(ang), jnp.sin(ang)
    return jnp.tile(cos, (1, 4)), jnp.concatenate([-sin, -sin, sin, sin], axis=1)


def _group(x, pos0, past_k, past_v, past_ki, state, w_all, g_pad, b_pad, t5_bias, log_gamma, fin_w,
           *, tm_proj, qb, sl, ret_chunk, tm_fin):
    bsz, t, _ = x.shape
    m = bsz * t
    x2d = x.reshape(m, D_MODEL)
    pos = pos0 + jnp.arange(t, dtype=jnp.int32)
    cos_t, sin_t = _rotary_tables(pos)
    n_pos_blocks = max(t // tm_proj, 1)
    if tm_proj > t:
        cos_t = jnp.tile(cos_t, (tm_proj // t, 1))
        sin_t = jnp.tile(sin_t, (tm_proj // t, 1))
    emit_vt = past_k is None and tm_proj == sl and t % sl == 0
    (qa, kf, kb, vf, vb, qi, kif, kib, wi, qr, kr, vr, gr, ga, gg) = _proj(
        x2d, w_all, cos_t, sin_t, g_pad, b_pad, tm=tm_proj, n_pos_blocks=n_pos_blocks, emit_vt=emit_vt)

    def b3(a):
        return a.reshape(bsz, t, a.shape[-1])

    n_keys = t if past_k is None else past_k.shape[1] + t
    lp = -(-n_keys // sl) * sl
    kb3, kib3 = b3(kb), b3(kib)
    if emit_vt:
        vt = vb.reshape(bsz, lp // sl, N_HEADS_A, VT_ROWS, sl)
    else:
        def assemble(past, new):
            parts = [] if past is None else [past.reshape(bsz, -1, new.shape[-1]).astype(BF16)]
            parts.append(new)
            if lp != n_keys:
                parts.append(jnp.zeros((bsz, lp - n_keys, new.shape[-1]), BF16))
            return jnp.concatenate(parts, axis=1) if len(parts) > 1 else new

        pki = None if past_ki is None else jnp.pad(past_ki, ((0, 0), (0, 0), (0, LANES - HEAD_DIM_IDX)))
        kb3, vb3, kib3 = assemble(past_k, kb3), assemble(past_v, b3(vb)), assemble(pki, kib3)
        vt = vb3.reshape(bsz, lp // sl, sl, W_A).transpose(0, 1, 3, 2)
        vt = vt.reshape(bsz, lp // sl, N_HEADS_A, HEAD_DIM_A, sl)
        vt = jnp.pad(vt, ((0, 0), (0, 0), (0, 0), (0, VT_ROWS - HEAD_DIM_A), (0, 0)), constant_values=1)
    qa3, qi3, wi3 = b3(qa), b3(qi), b3(wi)
    tq = -(-t // qb) * qb
    if tq != t:
        padq = ((0, 0), (0, tq - t), (0, 0))
        qa3, qi3, wi3 = jnp.pad(qa3, padq), jnp.pad(qi3, padq), jnp.pad(wi3, padq)
    a = _attention(qa3, qi3, wi3, kb3, vt, kib3, t5_bias, qb=qb, sl=sl, n_keys=n_keys, pos0=pos0)[:, :t]

    half = KEY_DIM_R // 2
    npair = N_HEADS_R // 2
    if state is None:
        s_init = jnp.zeros((bsz, N_HEADS_R, LANES, VAL_DIM_R), F32)
    else:
        st = state.astype(F32).reshape(bsz, npair, 2, 2, half, VAL_DIM_R)
        zero = jnp.zeros_like(st[:, :, 0])
        s_init = jnp.stack([jnp.stack([st[:, :, 0], zero], axis=3),
                            jnp.stack([zero, st[:, :, 1]], axis=3)], axis=2)
        s_init = s_init.reshape(bsz, N_HEADS_R, LANES, VAL_DIM_R)
    r, s_fin = _retention(b3(qr), b3(kr), b3(vr), b3(gr), fin_w["gn"], s_init, log_gamma, chunk=ret_chunk)
    s6 = s_fin.reshape(bsz, npair, 2, 2, 2, half, VAL_DIM_R)
    s_out = jnp.stack([s6[:, :, 0, :, 0], s6[:, :, 1, :, 1]], axis=2)
    s_out = s_out.reshape(bsz, N_HEADS_R, KEY_DIM_R, VAL_DIM_R)

    y = _finish(x2d, a.reshape(m, W_A), r.reshape(m, W_RV), ga, gg,
                fin_w["wpa"], fin_w["wpr"], fin_w["wo"], fin_w["ln1g"], fin_w["ln1b"],
                fin_w["wg"], fin_w["wu"], fin_w["wd"], fin_w["ln2g"], fin_w["ln2b"], tm=tm_fin)
    return (y.reshape(bsz, t, D_MODEL),
            kf.reshape(bsz, t, N_HEADS_A, HEAD_DIM_A), vf.reshape(bsz, t, N_HEADS_A, HEAD_DIM_A),
            kif.reshape(bsz, t, HEAD_DIM_IDX), s_out)


def kernel(x_prompt, x_sample, cache_k, cache_v, cache_idx_k, state_ret, w_in, idx_k_norm_g, idx_k_norm_b,
           t5_bias, ret_gn_g, w_pa, w_pr, w_o, ln1_g, ln1_b, w_gate, w_up, w_down, ln2_g, ln2_b):
    assert w_in.shape[0] == DEPTH
    log_gamma = jnp.log1p(-jnp.exp2(-5.0 - jnp.arange(N_HEADS_R, dtype=F32)))
    l = 0
    w_all = _relayout_w_in(w_in[l])
    lane_pad = (0, LANES - HEAD_DIM_IDX)
    g_pad = jnp.pad(idx_k_norm_g[l].astype(F32), lane_pad)[None, :]
    b_pad = jnp.pad(idx_k_norm_b[l].astype(F32), lane_pad)[None, :]
    fin_w = dict(
        gn=ret_gn_g[l][None, :].astype(F32),
        wpa=w_pa[l].astype(BF16), wpr=w_pr[l].astype(BF16), wo=w_o[l].astype(BF16),
        ln1g=ln1_g[l][None, :].astype(F32), ln1b=ln1_b[l][None, :].astype(F32),
        wg=w_gate[l].astype(BF16), wu=w_up[l].astype(BF16), wd=w_down[l].astype(BF16),
        ln2g=ln2_g[l][None, :].astype(F32), ln2b=ln2_b[l][None, :].astype(F32))

    t_p = x_prompt.shape[1]
    yp, kp, vp, ikp, sp = _group(
        x_prompt, 0, None, None, None, None, w_all, g_pad, b_pad, t5_bias, log_gamma, fin_w,
        tm_proj=min(512, t_p), qb=min(256, t_p), sl=512, ret_chunk=min(256, t_p), tm_fin=min(512, t_p))

    bs, t_s, _ = x_sample.shape
    past = cache_k.shape[2]
    ys, ks, vs, iks, ss = _group(
        x_sample, past, cache_k[l], cache_v[l], cache_idx_k[l], state_ret[l], w_all, g_pad, b_pad, t5_bias,
        log_gamma, fin_w, tm_proj=bs * t_s, qb=LANES, sl=512, ret_chunk=t_s, tm_fin=bs * t_s)

    return (yp, ys, kp[None], vp[None], ikp[None], sp[None],
            ks[None], vs[None], iks[None], ss.astype(state_ret.dtype)[None])
```
